```python
import jax, jax.numpy as jnp
from jax import lax
import numpy as np

D_MODEL = 1024
BATCH = 2
SEQ = 16384
DEPTH = 2

GRID_W = 64
CTX_LEN = 256
D_MIX = D_MODEL
RWKV_HEAD = 64
RWKV_WIDTH = D_MIX // 2
RWKV_HEADS = RWKV_WIDTH // RWKV_HEAD
DECAY_LORA = 64
ICLR_LORA = 64
GATE_LORA = 128
RWKV_COLS = 3 * RWKV_WIDTH + DECAY_LORA + ICLR_LORA + GATE_LORA
RWKV_SPLITS = (RWKV_WIDTH, 2 * RWKV_WIDTH, 3 * RWKV_WIDTH, 3 * RWKV_WIDTH + DECAY_LORA, 3 * RWKV_WIDTH + DECAY_LORA + ICLR_LORA)
SHIFT_TAPS = 3
POOL_WIDTH = D_MIX // 4
POOL_WINDOWS = (2, 4, 8, 16)
POOL_GROUPS = len(POOL_WINDOWS)
POOL_GC = POOL_WIDTH // POOL_GROUPS
FOURIER_WIDTH = D_MIX - RWKV_WIDTH - POOL_WIDTH
FOURIER_HEADS = 4
FOURIER_HC = FOURIER_WIDTH // FOURIER_HEADS
IN_COLS = RWKV_COLS + POOL_WIDTH + FOURIER_WIDTH
MIX_SPLITS = (RWKV_COLS, RWKV_COLS + POOL_WIDTH)
N_EXPERTS = 16
EXPERT_FF = D_MODEL
CAPACITY_FACTOR = 2
N_MOD = 6
RMS_EPS = 1e-6
LNX_EPS = 64e-5
F32 = jnp.float32

kernel_name = "hybrid_rwkv7_pool_fourier_ec_moe_dit"


def heads(t):
    return t.reshape(t.shape[0], t.shape[1], RWKV_HEADS, RWKV_HEAD)


def rmsnorm(x, g):
    xf = x.astype(F32)
    y = xf * lax.rsqrt(jnp.mean(xf * xf, axis=-1, keepdims=True) + RMS_EPS)
    return (y * g.astype(F32)).astype(x.dtype)


def adaln(cond, ada_w, ada_b):
    return jnp.split(jax.nn.silu(cond) @ ada_w + ada_b, N_MOD, axis=-1)


def modulate(x, shift, scale):
    return x * (1 + scale) + shift


def centred_shift(u, shift_w):
    up = jnp.pad(u, ((0, 0), (1, 1), (0, 0)))
    return up[:, :-2] * shift_w[0] + up[:, 1:-1] * shift_w[1] + up[:, 2:] * shift_w[2]


def rwkv_features(u, shift_w, k_k):
    u = centred_shift(u, shift_w)
    r, k, v, xw, xa, xg = jnp.split(u, RWKV_SPLITS, axis=-1)
    kk = heads(k * k_k).astype(F32)
    kk = kk * lax.rsqrt(jnp.sum(kk * kk, axis=-1, keepdims=True) + 1e-12)
    return r, k, v, xw, xa, xg, kk


def rwkv_direction(k, xw, xa, kk, w0, w_up, a0, a_up, k_a):
    w_log = -jax.nn.softplus(-(w0 + jnp.tanh(xw) @ w_up).astype(F32)) - 0.5
    decay = jnp.exp(-jnp.exp(w_log))
    a_lr = jax.nn.sigmoid(a0 + xa @ a_up)
    k_dir = k * (1 + (a_lr - 1) * k_a)
    return heads(decay), heads(k_dir), -kk, kk * heads(a_lr).astype(F32)


def delta_scan(r, w, k, v, a, b, s0, reverse, with_y):
    seqs = (w, k, v, a, b) + ((r,) if with_y else ())
    xs = tuple(jnp.moveaxis(t.astype(F32), 1, 0) for t in seqs)

    def step(S, inp):
        w_t, k_t, v_t, a_t, b_t = inp[:5]
        sa = jnp.einsum("bhvk,bhk->bhv", S, a_t)
        S = S * w_t[:, :, None, :] + sa[..., None] * b_t[:, :, None, :] + v_t[..., None] * k_t[:, :, None, :]
        y = jnp.einsum("bhvk,bhk->bhv", S, inp[5]) if with_y else None
        return S, y

    S, ys = lax.scan(step, s0, xs, reverse=reverse)
    return (jnp.moveaxis(ys, 0, 1) if with_y else None), S


def rwkv_readout(y, bonus, v, xg, gate_up, lnx_g, lnx_b):
    B, T = y.shape[0], y.shape[1]
    mu = jnp.mean(y, axis=-1, keepdims=True)
    var = jnp.mean(jnp.square(y - mu), axis=-1, keepdims=True)
    yn = ((y - mu) * lax.rsqrt(var + LNX_EPS)).reshape(B, T, RWKV_WIDTH) * lnx_g + lnx_b
    yn = yn + (bonus[..., None] * heads(v).astype(F32)).reshape(B, T, RWKV_WIDTH)
    return (yn * (jax.nn.sigmoid(xg) @ gate_up)).astype(v.dtype)


def rwkv_mixer(u_lat, u_ctx, ctx_out, shift_w, decay_w0, decay_up, iclr_a0, iclr_up, gate_up, k_k, k_a, r_k, lnx_g, lnx_b):
    r_l, k_l, v_l, xw_l, xa_l, xg_l, kk_l = rwkv_features(u_lat, shift_w, k_k)
    r_c, k_c, v_c, xw_c, xa_c, xg_c, kk_c = rwkv_features(u_ctx, shift_w, k_k)
    s_zero = jnp.zeros((u_ctx.shape[0], RWKV_HEADS, RWKV_HEAD, RWKV_HEAD), F32)
    y_lat, bonus_lat, y_ctx, bonus_ctx = [], [], [], []
    for d in range(2):
        rev = d == 1
        w_c, kd_c, a_c, b_c = rwkv_direction(k_c, xw_c, xa_c, kk_c, decay_w0[d], decay_up[d], iclr_a0[d], iclr_up[d], k_a)
        yc, s_ctx = delta_scan(heads(r_c), w_c, kd_c, heads(v_c), a_c, b_c, s_zero, rev, ctx_out)
        w_l, kd_l, a_l, b_l = rwkv_direction(k_l, xw_l, xa_l, kk_l, decay_w0[d], decay_up[d], iclr_a0[d], iclr_up[d], k_a)
        yl, _ = delta_scan(heads(r_l), w_l, kd_l, heads(v_l), a_l, b_l, s_ctx, rev, True)
        y_lat.append(yl)
        bonus_lat.append(jnp.sum(heads(r_l) * kd_l * r_k, axis=-1))
        if ctx_out:
            y_ctx.append(yc)
            bonus_ctx.append(jnp.sum(heads(r_c) * kd_c * r_k, axis=-1))
    out_lat = rwkv_readout(y_lat[0] + y_lat[1], bonus_lat[0] + bonus_lat[1], v_l, xg_l, gate_up, lnx_g, lnx_b)
    out_ctx = rwkv_readout(y_ctx[0] + y_ctx[1], bonus_ctx[0] + bonus_ctx[1], v_c, xg_c, gate_up, lnx_g, lnx_b) if ctx_out else None
    return out_lat, out_ctx


def window_bounds(n, win):
    pos = jnp.arange(n)
    return jnp.maximum(pos - win // 2, 0), jnp.minimum(pos + win // 2, n)


def pool2d_minus_self(u, rows):
    B, T, C = u.shape
    grid = u.reshape(B, rows, GRID_W, C).astype(F32)
    sat = jnp.pad(jnp.cumsum(jnp.cumsum(grid, axis=1), axis=2), ((0, 0), (1, 0), (1, 0), (0, 0)))
    means = []
    for gi, win in enumerate(POOL_WINDOWS):
        s = sat[..., gi * POOL_GC:(gi + 1) * POOL_GC]
        r_lo, r_hi = window_bounds(rows, win)
        c_lo, c_hi = window_bounds(GRID_W, win)
        s_hi, s_lo = s[:, r_hi], s[:, r_lo]
        box = s_hi[:, :, c_hi] - s_lo[:, :, c_hi] - s_hi[:, :, c_lo] + s_lo[:, :, c_lo]
        count = ((r_hi - r_lo)[:, None] * (c_hi - c_lo)[None, :]).astype(F32)
        means.append(box / count[None, :, :, None])
    pooled = jnp.concatenate(means, axis=-1).reshape(B, T, C)
    return (pooled - u.astype(F32)).astype(u.dtype)


def pool1d_minus_self(u):
    B, T, C = u.shape
    cs = jnp.pad(jnp.cumsum(u.astype(F32), axis=1), ((0, 0), (1, 0), (0, 0)))
    means = []
    for gi, win in enumerate(POOL_WINDOWS):
        s = cs[..., gi * POOL_GC:(gi + 1) * POOL_GC]
        lo, hi = window_bounds(T, win)
        means.append((s[:, hi] - s[:, lo]) / (hi - lo).astype(F32)[None, :, None])
    return (jnp.concatenate(means, axis=-1) - u.astype(F32)).astype(u.dtype)


def pool_readout(p, pool_w, pool_scale):
    B, T = p.shape[0], p.shape[1]
    ph = p.reshape(B, T, POOL_GROUPS, POOL_GC)
    return jnp.einsum("btgc,gcd->btgd", ph, pool_w).reshape(B, T, POOL_WIDTH) * pool_scale


def fourier_mixer(u, fourier_w):
    B, T = u.shape[0], u.shape[1]
    uh = u.reshape(B, T, FOURIER_HEADS, FOURIER_HC).astype(F32)
    f = jnp.real(jnp.fft.fft2(uh, axes=(1, 3), norm="ortho")).astype(u.dtype)
    return jnp.einsum("bthc,hcd->bthd", f, fourier_w).reshape(B, T, FOURIER_WIDTH)


def token_mixing(xn_lat, xn_ctx, rows, ctx_out, w_in, shift_w, decay_w0, decay_up, iclr_a0, iclr_up, gate_up,
                 k_k, k_a, r_k, lnx_g, lnx_b, pool_w, pool_scale, fourier_w, w_out):
    p_lat = xn_lat @ w_in
    p_ctx = xn_ctx @ (w_in if ctx_out else w_in[:, :RWKV_COLS])
    u_rw_l, u_pool_l, u_four_l = jnp.split(p_lat, MIX_SPLITS, axis=-1)
    rw_l, rw_c = rwkv_mixer(u_rw_l, p_ctx[..., :RWKV_COLS], ctx_out, shift_w, decay_w0, decay_up, iclr_a0, iclr_up,
                            gate_up, k_k, k_a, r_k, lnx_g, lnx_b)
    out_lat = jnp.concatenate([rw_l, pool_readout(pool2d_minus_self(u_pool_l, rows), pool_w, pool_scale),
                               fourier_mixer(u_four_l, fourier_w)], axis=-1) @ w_out
    if not ctx_out:
        return out_lat, None
    _, u_pool_c, u_four_c = jnp.split(p_ctx, MIX_SPLITS, axis=-1)
    out_ctx = jnp.concatenate([rw_c, pool_readout(pool1d_minus_self(u_pool_c), pool_w, pool_scale),
                               fourier_mixer(u_four_c, fourier_w)], axis=-1) @ w_out
    return out_lat, out_ctx


def expert_choice_ffn(h, router_w, w_gate, w_up, w_down):
    B, T, D = h.shape
    cap = CAPACITY_FACTOR * T // N_EXPERTS
    affinity = jax.nn.softmax((h @ router_w).astype(F32), axis=-1)
    gates, idx = lax.top_k(jnp.swapaxes(affinity, 1, 2), cap)
    xe = jax.vmap(lambda hb, ib: hb[ib])(h, idx)
    hid = jax.nn.silu(jnp.einsum("becd,edf->becf", xe, w_gate)) * jnp.einsum("becd,edf->becf", xe, w_up)
    ye = jnp.einsum("becf,efd->becd", hid, w_down) * gates[..., None].astype(h.dtype)
    return jax.vmap(lambda yb, ib: jnp.zeros((T, D), yb.dtype).at[ib.reshape(-1)].add(yb.reshape(-1, D)))(ye, idx)


def setup_inputs(seed: int = 0) -> dict:
    key = jax.random.key(seed)
    ks = iter(jax.random.split(key, 32))
    L, D, W = DEPTH, D_MODEL, RWKV_WIDTH

    def nrm(shape, scale):
        return jax.random.normal(next(ks), shape, F32) * scale

    shift_base = jnp.array([0.25, 0.5, 0.25], F32)[None, :, None]
    return {
        "x": nrm((BATCH, SEQ, D), 1.0),
        "c": nrm((BATCH, D), 1.0),
        "ctx": nrm((BATCH, CTX_LEN, D), 1.0),
        "c_ctx": nrm((D,), 1.0),
        "ada_w": nrm((L, D, N_MOD * D), 0.5 * D ** -0.5),
        "ada_b": nrm((L, N_MOD * D), 0.01),
        "norm_mix_g": 1.0 + nrm((L, D), 0.02),
        "norm_ffn_g": 1.0 + nrm((L, D), 0.02),
        "w_in": nrm((L, D, IN_COLS), D ** -0.5),
        "shift_w": shift_base + nrm((L, SHIFT_TAPS, RWKV_COLS), 0.05),
        "decay_w0": jax.random.uniform(next(ks), (L, 2, W), F32, -6.0, 0.5),
        "decay_up": nrm((L, 2, DECAY_LORA, W), 0.1),
        "iclr_a0": nrm((L, 2, W), 0.1),
        "iclr_up": nrm((L, 2, ICLR_LORA, W), 0.1),
        "gate_up": nrm((L, GATE_LORA, W), GATE_LORA ** -0.5),
        "k_k": 0.85 + nrm((L, W), 0.05),
        "k_a": 1.0 + nrm((L, W), 0.05),
        "r_k": nrm((L, RWKV_HEADS, RWKV_HEAD), 0.1),
        "lnx_g": 1.0 + nrm((L, W), 0.02),
        "lnx_b": nrm((L, W), 0.01),
        "pool_w": nrm((L, POOL_GROUPS, POOL_GC, POOL_GC), POOL_GC ** -0.5),
        "pool_scale": 1.0 + nrm((L, POOL_WIDTH), 0.1),
        "fourier_w": nrm((L, FOURIER_HEADS, FOURIER_HC, FOURIER_HC), FOURIER_HC ** -0.5),
        "w_out": nrm((L, D_MIX, D), D_MIX ** -0.5),
        "router_w": nrm((L, D, N_EXPERTS), D ** -0.5),
        "exp_w_gate": nrm((L, N_EXPERTS, D, EXPERT_FF), D ** -0.5),
        "exp_w_up": nrm((L, N_EXPERTS, D, EXPERT_FF), D ** -0.5),
        "exp_w_down": nrm((L, N_EXPERTS, EXPERT_FF, D), EXPERT_FF ** -0.5),
        "final_norm_g": 1.0 + nrm((D,), 0.02),
    }


def reference(x, c, ctx, c_ctx, ada_w, ada_b, norm_mix_g, norm_ffn_g, w_in, shift_w, decay_w0, decay_up, iclr_a0,
              iclr_up, gate_up, k_k, k_a, r_k, lnx_g, lnx_b, pool_w, pool_scale, fourier_w, w_out, router_w,
              exp_w_gate, exp_w_up, exp_w_down, final_norm_g):
    rows = x.shape[1] // GRID_W
    h_lat, h_ctx = x, ctx
    for l in range(DEPTH):
        ctx_out = l < DEPTH - 1
        sh1_l, sc1_l, g1_l, sh2_l, sc2_l, g2_l = [m[:, None, :] for m in adaln(c, ada_w[l], ada_b[l])]
        sh1_c, sc1_c, g1_c, sh2_c, sc2_c, g2_c = adaln(c_ctx, ada_w[l], ada_b[l])
        mix_lat, mix_ctx = token_mixing(
            modulate(rmsnorm(h_lat, norm_mix_g[l]), sh1_l, sc1_l),
            modulate(rmsnorm(h_ctx, norm_mix_g[l]), sh1_c, sc1_c),
            rows, ctx_out, w_in[l], shift_w[l], decay_w0[l], decay_up[l], iclr_a0[l], iclr_up[l], gate_up[l],
            k_k[l], k_a[l], r_k[l], lnx_g[l], lnx_b[l], pool_w[l], pool_scale[l], fourier_w[l], w_out[l])
        h_lat = h_lat + g1_l * mix_lat
        h_lat = h_lat + g2_l * expert_choice_ffn(modulate(rmsnorm(h_lat, norm_ffn_g[l]), sh2_l, sc2_l),
                                                 router_w[l], exp_w_gate[l], exp_w_up[l], exp_w_down[l])
        if ctx_out:
            h_ctx = h_ctx + g1_c * mix_ctx
            h_ctx = h_ctx + g2_c * expert_choice_ffn(modulate(rmsnorm(h_ctx, norm_ffn_g[l]), sh2_c, sc2_c),
                                                     router_w[l], exp_w_gate[l], exp_w_up[l], exp_w_down[l])
    return rmsnorm(h_lat, final_norm_g)
```

```python
import functools

import numpy as np
import jax
import jax.numpy as jnp
from jax import lax
from jax.experimental import pallas as pl
from jax.experimental.pallas import tpu as pltpu

F32 = jnp.float32
BF16 = jnp.bfloat16

D_MODEL = 1024
DEPTH = 2
GRID_W = 64
RWKV_HEAD = 64
RWKV_WIDTH = 512
RWKV_HEADS = 8
DECAY_LORA = 64
ICLR_LORA = 64
GATE_LORA = 128
RWKV_COLS = 3 * RWKV_WIDTH + DECAY_LORA + ICLR_LORA + GATE_LORA
RWKV_SPLITS = (RWKV_WIDTH, 2 * RWKV_WIDTH, 3 * RWKV_WIDTH, 3 * RWKV_WIDTH + DECAY_LORA,
               3 * RWKV_WIDTH + DECAY_LORA + ICLR_LORA)
POOL_WIDTH = 256
POOL_WINDOWS = (2, 4, 8, 16)
POOL_GROUPS = 4
POOL_GC = 64
FOURIER_WIDTH = 256
FOURIER_HEADS = 4
FOURIER_HC = 64
MIX_SPLITS = (RWKV_COLS, RWKV_COLS + POOL_WIDTH)
N_EXPERTS = 16
CAPACITY_FACTOR = 2
N_MOD = 6
RMS_EPS = 1e-6
LNX_EPS = 64e-5

SUBLANES = 8
LANES = 128

SCAN_TC = 32
SCAN_NSEL = SCAN_TC // 2
SCAN_KSEL = 2 * 3 * SCAN_TC
SCAN_NOPS = 5
SCAN_LROWS = SCAN_NOPS * RWKV_HEAD
SCAN_SUBS = 8
SCAN_BLK = SCAN_TC * SCAN_SUBS
SCAN_G = 2
KGROUPS = RWKV_HEAD // SUBLANES


def _sublane_allsum(x):
    x = x + pltpu.roll(x, 4, 0)
    x = x + pltpu.roll(x, 2, 0)
    return x + pltpu.roll(x, 1, 0)


def _tree_sum(xs):
    xs = list(xs)
    while len(xs) > 1:
        xs = [xs[i] + xs[i + 1] for i in range(0, len(xs), 2)]
    return xs[0]


def _scan_kernel(l_ref, sel_ref, v_ref, y_ref, e_ref, s_ref):
    @pl.when(pl.program_id(1) == 0)
    def _():
        s_ref[...] = jnp.zeros_like(s_ref)

    def sub_body(sub, carry):
        for g in range(SCAN_G):
            lg = l_ref[g, sub]
            for n in range(SCAN_NSEL):
                e_ref[g, n] = jnp.dot(lg, sel_ref[n], preferred_element_type=F32)

        state0 = tuple(tuple(s_ref[g, SUBLANES * j:SUBLANES * (j + 1), :] for j in range(KGROUPS))
                       for g in range(SCAN_G))

        def pair_body(n, state):
            for s in range(2):
                t = sub * SCAN_TC + 2 * n + s
                new_state = []
                for g in range(SCAN_G):
                    st = state[g]

                    def col(o, j, g=g, s=s):
                        r0 = o * RWKV_HEAD + SUBLANES * j
                        return e_ref[g, n, r0:r0 + SUBLANES, s * LANES:(s + 1) * LANES]

                    sa = _sublane_allsum(_tree_sum(st[j] * col(1, j) for j in range(KGROUPS)))
                    v8 = jnp.broadcast_to(v_ref[g, pl.ds(t, 1), :], (SUBLANES, LANES))
                    nst = tuple(st[j] * col(0, j) + col(2, j) * sa + col(3, j) * v8 for j in range(KGROUPS))
                    y8 = _sublane_allsum(_tree_sum(nst[j] * col(4, j) for j in range(KGROUPS)))
                    y_ref[g, pl.ds(t, 1), :] = y8[0:1, :]
                    new_state.append(nst)
                state = tuple(new_state)
            return state

        state = lax.fori_loop(0, SCAN_NSEL, pair_body, state0)
        for g in range(SCAN_G):
            for j in range(KGROUPS):
                s_ref[g, SUBLANES * j:SUBLANES * (j + 1), :] = state[g][j]
        return carry

    lax.fori_loop(0, SCAN_SUBS, sub_body, 0)


def _scan_selector():
    sel = np.zeros((SCAN_NSEL, 2, 3, SCAN_TC, 2, LANES), np.float32)
    for n in range(SCAN_NSEL):
        for s in range(2):
            for j in range(2):
                sel[n, j, :, 2 * n + s, s, j * RWKV_HEAD:(j + 1) * RWKV_HEAD] = 1.0
    return jnp.asarray(sel.reshape(SCAN_NSEL, SCAN_KSEL, 2 * LANES), BF16)


def _delta_scan_chains(l_all, v_all):
    chains, ttot, _ = v_all.shape
    assert chains % SCAN_G == 0 and ttot % SCAN_BLK == 0
    return pl.pallas_call(
        _scan_kernel,
        grid=(chains // SCAN_G, ttot // SCAN_BLK),
        in_specs=[
            pl.BlockSpec((SCAN_G, SCAN_SUBS, SCAN_LROWS, SCAN_KSEL), lambda c, b: (c, b, 0, 0)),
            pl.BlockSpec((SCAN_NSEL, SCAN_KSEL, 2 * LANES), lambda c, b: (0, 0, 0)),
            pl.BlockSpec((SCAN_G, SCAN_BLK, LANES), lambda c, b: (c, b, 0)),
        ],
        out_specs=pl.BlockSpec((SCAN_G, SCAN_BLK, LANES), lambda c, b: (c, b, 0)),
        out_shape=jax.ShapeDtypeStruct((chains, ttot, LANES), F32),
        scratch_shapes=[
            pltpu.VMEM((SCAN_G, SCAN_NSEL, SCAN_LROWS, 2 * LANES), F32),
            pltpu.VMEM((SCAN_G, RWKV_HEAD, LANES), F32),
        ],
        compiler_params=pltpu.CompilerParams(
            dimension_semantics=("arbitrary", "arbitrary"),
            vmem_limit_bytes=40 * 1024 * 1024,
        ),
        name="delta_scan",
    )(l_all, _scan_selector(), v_all)


def _split3_exact(x):
    mask = jnp.uint32(0xFFFF0000)
    hi = lax.bitcast_convert_type(lax.bitcast_convert_type(x, jnp.uint32) & mask, F32)
    r1 = x - hi
    mid = lax.bitcast_convert_type(lax.bitcast_convert_type(r1, jnp.uint32) & mask, F32)
    lo = r1 - mid
    return jnp.stack([hi, mid, lo], axis=-1).astype(BF16)


def _heads(t):
    return t.reshape(t.shape[0], t.shape[1], RWKV_HEADS, RWKV_HEAD)


def _rmsnorm(x, g):
    y = x * lax.rsqrt(jnp.mean(x * x, axis=-1, keepdims=True) + RMS_EPS)
    return y * g


def _adaln(cond, ada_w, ada_b):
    return jnp.split(jax.nn.silu(cond) @ ada_w + ada_b, N_MOD, axis=-1)


def _modulate(x, shift, scale):
    return x * (1 + scale) + shift


def _centred_shift(u, shift_w):
    up = jnp.pad(u, ((0, 0), (1, 1), (0, 0)))
    return up[:, :-2] * shift_w[0] + up[:, 1:-1] * shift_w[1] + up[:, 2:] * shift_w[2]


def _rwkv_features(u, shift_w, k_k):
    u = _centred_shift(u, shift_w)
    r, k, v, xw, xa, xg = jnp.split(u, RWKV_SPLITS, axis=-1)
    kk = _heads(k * k_k)
    kk = kk * lax.rsqrt(jnp.sum(kk * kk, axis=-1, keepdims=True) + 1e-12)
    return r, k, v, xw, xa, xg, kk.reshape(k.shape)


def _rwkv_readout(y, bonus, v, xg, gate_up, lnx_g, lnx_b):
    B, T = y.shape[0], y.shape[1]
    mu = jnp.mean(y, axis=-1, keepdims=True)
    var = jnp.mean(jnp.square(y - mu), axis=-1, keepdims=True)
    yn = ((y - mu) * lax.rsqrt(var + LNX_EPS)).reshape(B, T, RWKV_WIDTH) * lnx_g + lnx_b
    yn = yn + (bonus[..., None] * _heads(v)).reshape(B, T, RWKV_WIDTH)
    return yn * (jax.nn.sigmoid(xg) @ gate_up)


def _rwkv_mixer(u_lat, u_ctx, ctx_out, shift_w, decay_w0, decay_up, iclr_a0, iclr_up, gate_up, k_k, k_a, r_k,
                lnx_g, lnx_b):
    B, T = u_lat.shape[0], u_lat.shape[1]
    TC = u_ctx.shape[1]
    feats_l = _rwkv_features(u_lat, shift_w, k_k)
    feats_c = _rwkv_features(u_ctx, shift_w, k_k)
    r, k, v, xw, xa, xg, kk = [jnp.concatenate([c, l], axis=1) for c, l in zip(feats_c, feats_l)]
    ttot = TC + T

    def in_dir_order(t, d):
        if d == 0:
            return t
        return jnp.concatenate([t[:, :TC][:, ::-1], t[:, TC:][:, ::-1]], axis=1)

    cols, vrows, bonus = [], [], []
    for d in range(2):
        w_log = -jax.nn.softplus(-(decay_w0[d] + jnp.tanh(xw) @ decay_up[d])) - 0.5
        decay = jnp.exp(-jnp.exp(w_log))
        a_lr = jax.nn.sigmoid(iclr_a0[d] + xa @ iclr_up[d])
        k_dir = k * (1 + (a_lr - 1) * k_a)
        bonus.append(jnp.sum(_heads(r) * _heads(k_dir) * r_k, axis=-1))
        ops = jnp.stack([decay, -kk, kk * a_lr, k_dir, r], axis=2)
        cols.append(in_dir_order(ops, d))
        vrows.append(in_dir_order(v, d))
    cols = jnp.stack(cols, axis=1)
    parts = _split3_exact(cols)
    parts = parts.reshape(B, 2, ttot // SCAN_TC, SCAN_TC, SCAN_NOPS, 4, 2, RWKV_HEAD, 3)
    l_all = parts.transpose(0, 5, 1, 2, 4, 7, 6, 8, 3).reshape(B * 8, ttot // SCAN_TC, SCAN_LROWS, SCAN_KSEL)
    v_all = jnp.stack(vrows, axis=1).reshape(B, 2, ttot, 4, LANES).transpose(0, 3, 1, 2, 4).reshape(B * 8, ttot, LANES)

    y_all = _delta_scan_chains(l_all, v_all)
    y_all = y_all.reshape(B, 4, 2, ttot, LANES)
    y_f = y_all[:, :, 0]
    y_b = in_dir_order(y_all[:, :, 1].reshape(B * 4, ttot, LANES), 1).reshape(B, 4, ttot, LANES)
    y = (y_f + y_b).transpose(0, 2, 1, 3).reshape(B, ttot, RWKV_HEADS, RWKV_HEAD)
    bonus = bonus[0] + bonus[1]
    out = _rwkv_readout(y, bonus, v, xg, gate_up, lnx_g, lnx_b)
    return out[:, TC:], (out[:, :TC] if ctx_out else None)


def _window_bounds(n, win):
    pos = jnp.arange(n)
    return jnp.maximum(pos - win // 2, 0), jnp.minimum(pos + win // 2, n)


def _pool2d_minus_self(u, rows):
    B, T, C = u.shape
    grid = u.reshape(B, rows, GRID_W, C)
    sat = jnp.pad(jnp.cumsum(jnp.cumsum(grid, axis=1), axis=2), ((0, 0), (1, 0), (1, 0), (0, 0)))
    means = []
    for gi, win in enumerate(POOL_WINDOWS):
        s = sat[..., gi * POOL_GC:(gi + 1) * POOL_GC]
        r_lo, r_hi = _window_bounds(rows, win)
        c_lo, c_hi = _window_bounds(GRID_W, win)
        s_hi, s_lo = s[:, r_hi], s[:, r_lo]
        box = s_hi[:, :, c_hi] - s_lo[:, :, c_hi] - s_hi[:, :, c_lo] + s_lo[:, :, c_lo]
        count = ((r_hi - r_lo)[:, None] * (c_hi - c_lo)[None, :]).astype(F32)
        means.append(box / count[None, :, :, None])
    pooled = jnp.concatenate(means, axis=-1).reshape(B, T, C)
    return pooled - u


def _pool1d_minus_self(u):
    B, T, C = u.shape
    cs = jnp.pad(jnp.cumsum(u, axis=1), ((0, 0), (1, 0), (0, 0)))
    means = []
    for gi, win in enumerate(POOL_WINDOWS):
        s = cs[..., gi * POOL_GC:(gi + 1) * POOL_GC]
        lo, hi = _window_bounds(T, win)
        means.append((s[:, hi] - s[:, lo]) / (hi - lo).astype(F32)[None, :, None])
    return jnp.concatenate(means, axis=-1) - u


def _pool_readout(p, pool_w, pool_scale):
    B, T = p.shape[0], p.shape[1]
    ph = p.reshape(B, T, POOL_GROUPS, POOL_GC)
    return jnp.einsum("btgc,gcd->btgd", ph, pool_w).reshape(B, T, POOL_WIDTH) * pool_scale


def _fourier_mixer(u, fourier_w):
    B, T = u.shape[0], u.shape[1]
    uh = u.reshape(B, T, FOURIER_HEADS, FOURIER_HC)
    f = jnp.real(jnp.fft.fft2(uh, axes=(1, 3), norm="ortho"))
    return jnp.einsum("bthc,hcd->bthd", f, fourier_w).reshape(B, T, FOURIER_WIDTH)


def _token_mixing(xn_lat, xn_ctx, rows, ctx_out, w_in, shift_w, decay_w0, decay_up, iclr_a0, iclr_up, gate_up,
                  k_k, k_a, r_k, lnx_g, lnx_b, pool_w, pool_scale, fourier_w, w_out):
    p_lat = xn_lat @ w_in
    p_ctx = xn_ctx @ (w_in if ctx_out else w_in[:, :RWKV_COLS])
    u_rw_l, u_pool_l, u_four_l = jnp.split(p_lat, MIX_SPLITS, axis=-1)
    rw_l, rw_c = _rwkv_mixer(u_rw_l, p_ctx[..., :RWKV_COLS], ctx_out, shift_w, decay_w0, decay_up, iclr_a0,
                             iclr_up, gate_up, k_k, k_a, r_k, lnx_g, lnx_b)
    out_lat = jnp.concatenate([rw_l, _pool_readout(_pool2d_minus_self(u_pool_l, rows), pool_w, pool_scale),
                               _fourier_mixer(u_four_l, fourier_w)], axis=-1) @ w_out
    if not ctx_out:
        return out_lat, None
    _, u_pool_c, u_four_c = jnp.split(p_ctx, MIX_SPLITS, axis=-1)
    out_ctx = jnp.concatenate([rw_c, _pool_readout(_pool1d_minus_self(u_pool_c), pool_w, pool_scale),
                               _fourier_mixer(u_four_c, fourier_w)], axis=-1) @ w_out
    return out_lat, out_ctx


def _expert_choice_ffn(h, router_w, w_gate, w_up, w_down):
    B, T, D = h.shape
    cap = CAPACITY_FACTOR * T // N_EXPERTS
    affinity = jax.nn.softmax(h @ router_w, axis=-1)
    gates, idx = lax.top_k(jnp.swapaxes(affinity, 1, 2), cap)
    xe = jax.vmap(lambda hb, ib: hb[ib])(h, idx)
    hid = jax.nn.silu(jnp.einsum("becd,edf->becf", xe, w_gate)) * jnp.einsum("becd,edf->becf", xe, w_up)
    ye = jnp.einsum("becf,efd->becd", hid, w_down) * gates[..., None]
    return jax.vmap(lambda yb, ib: jnp.zeros((T, D), yb.dtype).at[ib.reshape(-1)].add(yb.reshape(-1, D)))(ye, idx)


def kernel(x, c, ctx, c_ctx, ada_w, ada_b, norm_mix_g, norm_ffn_g, w_in, shift_w, decay_w0, decay_up, iclr_a0,
           iclr_up, gate_up, k_k, k_a, r_k, lnx_g, lnx_b, pool_w, pool_scale, fourier_w, w_out, router_w,
           exp_w_gate, exp_w_up, exp_w_down, final_norm_g):
    rows = x.shape[1] // GRID_W
    h_lat, h_ctx = x, ctx
    for l in range(DEPTH):
        ctx_out = l < DEPTH - 1
        sh1_l, sc1_l, g1_l, sh2_l, sc2_l, g2_l = [m[:, None, :] for m in _adaln(c, ada_w[l], ada_b[l])]
        sh1_c, sc1_c, g1_c, sh2_c, sc2_c, g2_c = _adaln(c_ctx, ada_w[l], ada_b[l])
        mix_lat, mix_ctx = _token_mixing(
            _modulate(_rmsnorm(h_lat, norm_mix_g[l]), sh1_l, sc1_l),
            _modulate(_rmsnorm(h_ctx, norm_mix_g[l]), sh1_c, sc1_c),
            rows, ctx_out, w_in[l], shift_w[l], decay_w0[l], decay_up[l], iclr_a0[l], iclr_up[l], gate_up[l],
            k_k[l], k_a[l], r_k[l], lnx_g[l], lnx_b[l], pool_w[l], pool_scale[l], fourier_w[l], w_out[l])
        h_lat = h_lat + g1_l * mix_lat
        h_lat = h_lat + g2_l * _expert_choice_ffn(_modulate(_rmsnorm(h_lat, norm_ffn_g[l]), sh2_l, sc2_l),
                                                  router_w[l], exp_w_gate[l], exp_w_up[l], exp_w_down[l])
        if ctx_out:
            h_ctx = h_ctx + g1_c * mix_ctx
            h_ctx = h_ctx + g2_c * _expert_choice_ffn(_modulate(_rmsnorm(h_ctx, norm_ffn_g[l]), sh2_c, sc2_c),
                                                      router_w[l], exp_w_gate[l], exp_w_up[l], exp_w_down[l])
    return _rmsnorm(h_lat, final_norm_g)
```

```python
import numpy as np
import jax
import jax.numpy as jnp
from jax import lax
from jax.experimental import pallas as pl
from jax.experimental.pallas import tpu as pltpu

F32 = jnp.float32
BF16 = jnp.bfloat16
HIGHEST = lax.Precision.HIGHEST

D_MODEL = 1024
DEPTH = 2
GRID_W = 64
CTX_LEN = 256
RWKV_HEAD = 64
RWKV_WIDTH = 512
RWKV_HEADS = 8
DECAY_LORA = 64
ICLR_LORA = 64
GATE_LORA = 128
RWKV_COLS = 3 * RWKV_WIDTH + DECAY_LORA + ICLR_LORA + GATE_LORA
POOL_WIDTH = 256
POOL_WINDOWS = (2, 4, 8, 16)
POOL_GROUPS = 4
POOL_GC = 64
FOURIER_WIDTH = 256
FOURIER_HEADS = 4
FOURIER_HC = 64
IN_COLS = RWKV_COLS + POOL_WIDTH + FOURIER_WIDTH
MIX_SPLITS = (RWKV_COLS, RWKV_COLS + POOL_WIDTH)
N_EXPERTS = 16
CAPACITY_FACTOR = 2
N_MOD = 6
RMS_EPS = 1e-6
LNX_EPS = 64e-5

SUBLANES = 8
LANES = 128
BF16_ROWS = 16

SCAN_TC = 32
SCAN_SUBS = 8
TBLK = SCAN_TC * SCAN_SUBS
SCAN_NSEL = SCAN_TC // 2
SCAN_PARTS = 4
SCAN_KSEL = 2 * SCAN_PARTS * SCAN_TC
SCAN_NOPS = 5
SCAN_LROWS = SCAN_NOPS * RWKV_HEAD
KGROUPS = RWKV_HEAD // SUBLANES
HEAD_PAIRS = RWKV_HEADS // 2
assert TBLK == CTX_LEN and SCAN_PARTS * SCAN_TC == LANES


def _mirror_block(i, nblk):
    return jnp.where(i == 0, 0, nblk - i)


def _split3_exact(x):
    mask = jnp.uint32(0xFFFF0000)
    hi = pltpu.bitcast(pltpu.bitcast(x, jnp.uint32) & mask, F32)
    r1 = x - hi
    mid = pltpu.bitcast(pltpu.bitcast(r1, jnp.uint32) & mask, F32)
    return hi, mid, r1 - mid


def _sigmoid(x):
    return 1.0 / (1.0 + jnp.exp(-x))


def _softplus(x):
    return jnp.maximum(x, 0.0) + jnp.log(1.0 + jnp.exp(-jnp.abs(x)))


def _prep_kernel(p_ref, prev_ref, next_ref, sw_ref, kk_ref, ka_ref, rk_ref, w0_ref, a0_ref, dup_ref, aup_ref,
                 gup_ref, bd_ref, lt0_ref, lt1_ref, v_ref, gate_ref, bonus_ref):
    i = pl.program_id(1)
    nblk = pl.num_programs(1)
    prev_ok = i >= 2
    next_ok = jnp.logical_and(i >= 1, i <= nblk - 2)
    row = lax.broadcasted_iota(jnp.int32, (TBLK, 1), 0)

    def section(c0, c1):
        x = p_ref[0, :, c0:c1]
        prev_row = jnp.where(prev_ok, prev_ref[0, SUBLANES - 1:SUBLANES, c0:c1], 0.0)
        next_row = jnp.where(next_ok, next_ref[0, 0:1, c0:c1], 0.0)
        x_prev = jnp.where(row == 0, prev_row, pltpu.roll(x, 1, 0))
        x_next = jnp.where(row == TBLK - 1, next_row, pltpu.roll(x, TBLK - 1, 0))
        return x_prev * sw_ref[0:1, c0:c1] + x * sw_ref[1:2, c0:c1] + x_next * sw_ref[2:3, c0:c1]

    W = RWKV_WIDTH
    r = section(0, W)
    k = section(W, 2 * W)
    v = section(2 * W, 3 * W)
    xwa = section(3 * W, 3 * W + LANES)
    xg = section(3 * W + LANES, 3 * W + 2 * LANES)
    bd = bd_ref[...]

    kk = k * kk_ref[...]
    kk = kk * lax.rsqrt(jnp.dot(kk * kk, bd, precision=HIGHEST, preferred_element_type=F32) + 1e-12)
    v_ref[0] = v
    gate_ref[0] = jnp.dot(_sigmoid(xg), gup_ref[...], precision=HIGHEST, preferred_element_type=F32)
    txwa = jnp.tanh(xwa)
    a_neg = -kk

    def store_parts(lt_ref, d, o, x):
        parts = _split3_exact(x)
        for s in range(SCAN_SUBS):
            slot = s if d == 0 else SCAN_SUBS - 1 - s
            for p in range(3):
                lt_ref[0, slot, o, p * SCAN_TC:(p + 1) * SCAN_TC, :] = (
                    parts[p][s * SCAN_TC:(s + 1) * SCAN_TC, :].astype(BF16))
            lt_ref[0, slot, o, 3 * SCAN_TC:, :] = jnp.zeros((SCAN_TC, W), BF16)

    bonus = None
    for d, lt_ref in ((0, lt0_ref), (1, lt1_ref)):
        lw = jnp.dot(txwa, dup_ref[d], precision=HIGHEST, preferred_element_type=F32)
        w_log = -_softplus(-(w0_ref[d:d + 1, :] + lw)) - 0.5
        decay = jnp.exp(-jnp.exp(w_log))
        a_lr = _sigmoid(a0_ref[d:d + 1, :] + jnp.dot(xwa, aup_ref[d], precision=HIGHEST, preferred_element_type=F32))
        k_dir = k * (1.0 + (a_lr - 1.0) * ka_ref[...])
        bterm = jnp.dot(r * k_dir * rk_ref[...], bd, precision=HIGHEST, preferred_element_type=F32)
        bonus = bterm if bonus is None else bonus + bterm
        store_parts(lt_ref, d, 0, decay)
        store_parts(lt_ref, d, 1, a_neg)
        store_parts(lt_ref, d, 2, kk * a_lr)
        store_parts(lt_ref, d, 3, k_dir)
        store_parts(lt_ref, d, 4, r)
    bonus_ref[0] = bonus


def _rwkv_prep(p_all, shift_w, k_k, k_a, r_k, decay_w0, decay_up, iclr_a0, iclr_up, gate_up):
    B, TT, _ = p_all.shape
    nblk = TT // TBLK
    W = RWKV_WIDTH
    zeros = jnp.zeros((2, DECAY_LORA, W), F32)
    dup_pad = jnp.concatenate([decay_up, zeros], axis=1)
    aup_pad = jnp.concatenate([zeros, iclr_up], axis=1)
    bd = jnp.asarray(np.kron(np.eye(RWKV_HEADS, dtype=np.float32), np.ones((RWKV_HEAD, RWKV_HEAD), np.float32)))
    rows8 = TT // SUBLANES
    full = lambda shape: pl.BlockSpec(shape, lambda b, i: (0,) * len(shape))
    lt_shape = jax.ShapeDtypeStruct((B, nblk * SCAN_SUBS, SCAN_NOPS, LANES, W), BF16)
    tok_shape = jax.ShapeDtypeStruct((B, TT, W), F32)
    tok_spec = pl.BlockSpec((1, TBLK, W), lambda b, i: (b, i, 0))
    return pl.pallas_call(
        _prep_kernel,
        grid=(B, nblk),
        in_specs=[
            pl.BlockSpec((1, TBLK, RWKV_COLS), lambda b, i: (b, i, 0)),
            pl.BlockSpec((1, SUBLANES, RWKV_COLS), lambda b, i: (b, jnp.maximum(i * (TBLK // SUBLANES) - 1, 0), 0)),
            pl.BlockSpec((1, SUBLANES, RWKV_COLS),
                         lambda b, i: (b, jnp.minimum((i + 1) * (TBLK // SUBLANES), rows8 - 1), 0)),
            full((3, RWKV_COLS)), full((1, W)), full((1, W)), full((1, W)), full((2, W)), full((2, W)),
            full((2, LANES, W)), full((2, LANES, W)), full((GATE_LORA, W)), full((W, W)),
        ],
        out_specs=[
            pl.BlockSpec((1, SCAN_SUBS, SCAN_NOPS, LANES, W), lambda b, i: (b, i, 0, 0, 0)),
            pl.BlockSpec((1, SCAN_SUBS, SCAN_NOPS, LANES, W), lambda b, i: (b, _mirror_block(i, nblk), 0, 0, 0)),
            tok_spec, tok_spec, tok_spec,
        ],
        out_shape=[lt_shape, lt_shape, tok_shape, tok_shape, tok_shape],
        compiler_params=pltpu.CompilerParams(
            dimension_semantics=("arbitrary", "arbitrary"),
            vmem_limit_bytes=48 * 1024 * 1024,
        ),
        name="rwkv_prep",
    )(p_all, p_all, p_all, shift_w, k_k.reshape(1, W), k_a.reshape(1, W), r_k.reshape(1, W), decay_w0, iclr_a0,
      dup_pad, aup_pad, gate_up, bd)


def _sublane_allsum(x):
    x = x + pltpu.roll(x, 4, 0)
    x = x + pltpu.roll(x, 2, 0)
    return x + pltpu.roll(x, 1, 0)


def _tree_sum(xs):
    xs = list(xs)
    while len(xs) > 1:
        xs = [xs[i] + xs[i + 1] for i in range(0, len(xs), 2)]
    return xs[0]


def _scan_kernel(lt0_ref, lt1_ref, ltn0_ref, ltn1_ref, sel_ref, vf_ref, vb_ref, yf_ref, yb_ref,
                 e0_ref, e1_ref, l_ref, s_ref):
    i = pl.program_id(1)
    lt_refs = (lt0_ref, lt1_ref)
    ltn_refs = (ltn0_ref, ltn1_ref)
    v_refs = (vf_ref, vb_ref)
    y_refs = (yf_ref, yb_ref)

    def build_l(g, src_ref, idx):
        for o in range(SCAN_NOPS):
            xt = src_ref[0, idx, o].T
            l_ref[g, o * RWKV_HEAD:(o + 1) * RWKV_HEAD, 0:LANES] = xt[0:RWKV_HEAD]
            l_ref[g, o * RWKV_HEAD:(o + 1) * RWKV_HEAD, LANES:2 * LANES] = xt[RWKV_HEAD:2 * RWKV_HEAD]

    def expand(e_ref, n):
        for g in range(2):
            e_ref[g, n] = jnp.dot(l_ref[g], sel_ref[g, n], preferred_element_type=F32)

    @pl.when(i == 0)
    def _():
        s_ref[...] = jnp.zeros_like(s_ref)
        for g in range(2):
            build_l(g, lt_refs[g], 0)
        for n in range(SCAN_NSEL):
            expand(e0_ref, n)

    def scan_sub(sub, e_cur, e_nxt):
        @pl.when(sub < SCAN_SUBS - 1)
        def _():
            for g in range(2):
                build_l(g, lt_refs[g], sub + 1)

        @pl.when(sub == SCAN_SUBS - 1)
        def _():
            for g in range(2):
                build_l(g, ltn_refs[g], 0)

        state = [[s_ref[g, SUBLANES * j:SUBLANES * (j + 1), :] for j in range(KGROUPS)] for g in range(2)]
        for n in range(SCAN_NSEL):
            expand(e_nxt, n)
            for s in range(2):
                t = sub * SCAN_TC + 2 * n + s
                for g in range(2):
                    st = state[g]
                    trow = t if g == 0 else TBLK - 1 - t

                    def col(o, j, g=g, s=s, n=n):
                        r0 = o * RWKV_HEAD + SUBLANES * j
                        return e_cur[g, n, r0:r0 + SUBLANES, s * LANES:(s + 1) * LANES]

                    sa = _sublane_allsum(_tree_sum(st[j] * col(1, j) for j in range(KGROUPS)))
                    v8 = jnp.broadcast_to(v_refs[g][0, pl.ds(trow, 1), :], (SUBLANES, LANES))
                    nst = [st[j] * col(0, j) + col(2, j) * sa + col(3, j) * v8 for j in range(KGROUPS)]
                    y8 = _sublane_allsum(_tree_sum(nst[j] * col(4, j) for j in range(KGROUPS)))
                    y_refs[g][0, pl.ds(trow, 1), :] = y8[0:1, :]
                    state[g] = nst
        for g in range(2):
            for j in range(KGROUPS):
                s_ref[g, SUBLANES * j:SUBLANES * (j + 1), :] = state[g][j]

    def sub_pair(q, carry):
        scan_sub(2 * q, e0_ref, e1_ref)
        scan_sub(2 * q + 1, e1_ref, e0_ref)
        return carry

    lax.fori_loop(0, SCAN_SUBS // 2, sub_pair, 0)


def _scan_selector():
    sel = np.zeros((2, SCAN_NSEL, 2, SCAN_PARTS, SCAN_TC, 2, LANES), np.float32)
    for g in range(2):
        for n in range(SCAN_NSEL):
            for s in range(2):
                step = 2 * n + s if g == 0 else SCAN_TC - 1 - (2 * n + s)
                for j in range(2):
                    sel[g, n, j, :3, step, s, j * RWKV_HEAD:(j + 1) * RWKV_HEAD] = 1.0
    return jnp.asarray(sel.reshape(2, SCAN_NSEL, SCAN_KSEL, 2 * LANES), BF16)


def _delta_scan(lt0, lt1, v_all):
    B, TT, W = v_all.shape
    nblk = TT // TBLK
    nsub = nblk * SCAN_SUBS
    lt_spec = pl.BlockSpec((1, SCAN_SUBS, SCAN_NOPS, LANES, LANES), lambda c, i: (c // HEAD_PAIRS, i, 0, 0, c % HEAD_PAIRS))
    ltn_spec = pl.BlockSpec((1, 1, SCAN_NOPS, LANES, LANES),
                            lambda c, i: (c // HEAD_PAIRS, jnp.minimum((i + 1) * SCAN_SUBS, nsub - 1), 0, 0, c % HEAD_PAIRS))
    f_spec = pl.BlockSpec((1, TBLK, LANES), lambda c, i: (c // HEAD_PAIRS, i, c % HEAD_PAIRS))
    b_spec = pl.BlockSpec((1, TBLK, LANES), lambda c, i: (c // HEAD_PAIRS, _mirror_block(i, nblk), c % HEAD_PAIRS))
    y_shape = jax.ShapeDtypeStruct((B, TT, W), F32)
    return pl.pallas_call(
        _scan_kernel,
        grid=(B * HEAD_PAIRS, nblk),
        in_specs=[lt_spec, lt_spec, ltn_spec, ltn_spec,
                  pl.BlockSpec((2, SCAN_NSEL, SCAN_KSEL, 2 * LANES), lambda c, i: (0, 0, 0, 0)),
                  f_spec, b_spec],
        out_specs=[f_spec, b_spec],
        out_shape=[y_shape, y_shape],
        scratch_shapes=[
            pltpu.VMEM((2, SCAN_NSEL, SCAN_LROWS, 2 * LANES), F32),
            pltpu.VMEM((2, SCAN_NSEL, SCAN_LROWS, 2 * LANES), F32),
            pltpu.VMEM((2, SCAN_LROWS, SCAN_KSEL), BF16),
            pltpu.VMEM((2, RWKV_HEAD, LANES), F32),
        ],
        compiler_params=pltpu.CompilerParams(
            dimension_semantics=("arbitrary", "arbitrary"),
            vmem_limit_bytes=52 * 1024 * 1024,
        ),
        name="delta_scan",
    )(lt0, lt1, lt0, lt1, _scan_selector(), v_all, v_all)


def _heads(t):
    return t.reshape(t.shape[0], t.shape[1], RWKV_HEADS, RWKV_HEAD)


def _rmsnorm(x, g):
    y = x * lax.rsqrt(jnp.mean(x * x, axis=-1, keepdims=True) + RMS_EPS)
    return y * g


def _adaln(cond, ada_w, ada_b):
    return jnp.split(jax.nn.silu(cond) @ ada_w + ada_b, N_MOD, axis=-1)


def _modulate(x, shift, scale):
    return x * (1 + scale) + shift


def _rwkv_mixer(p_all, ctx_out, shift_w, decay_w0, decay_up, iclr_a0, iclr_up, gate_up, k_k, k_a, r_k, lnx_g, lnx_b):
    B, TT = p_all.shape[0], p_all.shape[1]
    lt0, lt1, v, gate, bonus = _rwkv_prep(p_all, shift_w[:, :RWKV_COLS], k_k, k_a, r_k, decay_w0, decay_up, iclr_a0,
                                          iclr_up, gate_up)
    y_f, y_b = _delta_scan(lt0, lt1, v)
    y = _heads(y_f + y_b)
    mu = jnp.mean(y, axis=-1, keepdims=True)
    var = jnp.mean(jnp.square(y - mu), axis=-1, keepdims=True)
    yn = ((y - mu) * lax.rsqrt(var + LNX_EPS)).reshape(B, TT, RWKV_WIDTH) * lnx_g + lnx_b
    out = (yn + bonus * v) * gate
    return out[:, CTX_LEN:], (out[:, :CTX_LEN] if ctx_out else None)


def _window_bounds(n, win):
    pos = jnp.arange(n)
    return jnp.maximum(pos - win // 2, 0), jnp.minimum(pos + win // 2, n)


def _pool2d_minus_self(u, rows):
    B, T, C = u.shape
    grid = u.reshape(B, rows, GRID_W, C)
    sat = jnp.pad(jnp.cumsum(jnp.cumsum(grid, axis=1), axis=2), ((0, 0), (1, 0), (1, 0), (0, 0)))
    means = []
    for gi, win in enumerate(POOL_WINDOWS):
        s = sat[..., gi * POOL_GC:(gi + 1) * POOL_GC]
        r_lo, r_hi = _window_bounds(rows, win)
        c_lo, c_hi = _window_bounds(GRID_W, win)
        s_hi, s_lo = s[:, r_hi], s[:, r_lo]
        box = s_hi[:, :, c_hi] - s_lo[:, :, c_hi] - s_hi[:, :, c_lo] + s_lo[:, :, c_lo]
        count = ((r_hi - r_lo)[:, None] * (c_hi - c_lo)[None, :]).astype(F32)
        means.append(box / count[None, :, :, None])
    pooled = jnp.concatenate(means, axis=-1).reshape(B, T, C)
    return pooled - u


def _pool1d_minus_self(u):
    B, T, C = u.shape
    cs = jnp.pad(jnp.cumsum(u, axis=1), ((0, 0), (1, 0), (0, 0)))
    means = []
    for gi, win in enumerate(POOL_WINDOWS):
        s = cs[..., gi * POOL_GC:(gi + 1) * POOL_GC]
        lo, hi = _window_bounds(T, win)
        means.append((s[:, hi] - s[:, lo]) / (hi - lo).astype(F32)[None, :, None])
    return jnp.concatenate(means, axis=-1) - u


def _pool_readout(p, pool_w, pool_scale):
    B, T = p.shape[0], p.shape[1]
    ph = p.reshape(B, T, POOL_GROUPS, POOL_GC)
    return jnp.einsum("btgc,gcd->btgd", ph, pool_w).reshape(B, T, POOL_WIDTH) * pool_scale


def _fourier_mixer(u, fourier_w):
    B, T = u.shape[0], u.shape[1]
    uh = u.reshape(B, T, FOURIER_HEADS, FOURIER_HC)
    f = jnp.real(jnp.fft.fft2(uh, axes=(1, 3), norm="ortho"))
    return jnp.einsum("bthc,hcd->bthd", f, fourier_w).reshape(B, T, FOURIER_WIDTH)


def _token_mixing(xn_lat, xn_ctx, rows, ctx_out, w_in, shift_w, decay_w0, decay_up, iclr_a0, iclr_up, gate_up,
                  k_k, k_a, r_k, lnx_g, lnx_b, pool_w, pool_scale, fourier_w, w_out):
    p_lat = xn_lat @ w_in
    p_ctx = xn_ctx @ w_in
    p_all = jnp.concatenate([p_ctx, p_lat], axis=1)
    rw_l, rw_c = _rwkv_mixer(p_all, ctx_out, shift_w, decay_w0, decay_up, iclr_a0, iclr_up, gate_up, k_k, k_a, r_k,
                             lnx_g, lnx_b)
    _, u_pool_l, u_four_l = jnp.split(p_lat, MIX_SPLITS, axis=-1)
    out_lat = jnp.concatenate([rw_l, _pool_readout(_pool2d_minus_self(u_pool_l, rows), pool_w, pool_scale),
                               _fourier_mixer(u_four_l, fourier_w)], axis=-1) @ w_out
    if not ctx_out:
        return out_lat, None
    _, u_pool_c, u_four_c = jnp.split(p_ctx, MIX_SPLITS, axis=-1)
    out_ctx = jnp.concatenate([rw_c, _pool_readout(_pool1d_minus_self(u_pool_c), pool_w, pool_scale),
                               _fourier_mixer(u_four_c, fourier_w)], axis=-1) @ w_out
    return out_lat, out_ctx


def _expert_choice_ffn(h, router_w, w_gate, w_up, w_down):
    B, T, D = h.shape
    cap = CAPACITY_FACTOR * T // N_EXPERTS
    affinity = jax.nn.softmax(h @ router_w, axis=-1)
    gates, idx = lax.top_k(jnp.swapaxes(affinity, 1, 2), cap)
    xe = jax.vmap(lambda hb, ib: hb[ib])(h, idx)
    hid = jax.nn.silu(jnp.einsum("becd,edf->becf", xe, w_gate)) * jnp.einsum("becd,edf->becf", xe, w_up)
    ye = jnp.einsum("becf,efd->becd", hid, w_down) * gates[..., None]
    return jax.vmap(lambda yb, ib: jnp.zeros((T, D), yb.dtype).at[ib.reshape(-1)].add(yb.reshape(-1, D)))(ye, idx)


def kernel(x, c, ctx, c_ctx, ada_w, ada_b, norm_mix_g, norm_ffn_g, w_in, shift_w, decay_w0, decay_up, iclr_a0,
           iclr_up, gate_up, k_k, k_a, r_k, lnx_g, lnx_b, pool_w, pool_scale, fourier_w, w_out, router_w,
           exp_w_gate, exp_w_up, exp_w_down, final_norm_g):
    rows = x.shape[1] // GRID_W
    h_lat, h_ctx = x, ctx
    for l in range(DEPTH):
        ctx_out = l < DEPTH - 1
        sh1_l, sc1_l, g1_l, sh2_l, sc2_l, g2_l = [m[:, None, :] for m in _adaln(c, ada_w[l], ada_b[l])]
        sh1_c, sc1_c, g1_c, sh2_c, sc2_c, g2_c = _adaln(c_ctx, ada_w[l], ada_b[l])
        mix_lat, mix_ctx = _token_mixing(
            _modulate(_rmsnorm(h_lat, norm_mix_g[l]), sh1_l, sc1_l),
            _modulate(_rmsnorm(h_ctx, norm_mix_g[l]), sh1_c, sc1_c),
            rows, ctx_out, w_in[l], shift_w[l], decay_w0[l], decay_up[l], iclr_a0[l], iclr_up[l], gate_up[l],
            k_k[l], k_a[l], r_k[l], lnx_g[l], lnx_b[l], pool_w[l], pool_scale[l], fourier_w[l], w_out[l])
        h_lat = h_lat + g1_l * mix_lat
        h_lat = h_lat + g2_l * _expert_choice_ffn(_modulate(_rmsnorm(h_lat, norm_ffn_g[l]), sh2_l, sc2_l),
                                                  router_w[l], exp_w_gate[l], exp_w_up[l], exp_w_down[l])
        if ctx_out:
            h_ctx = h_ctx + g1_c * mix_ctx
            h_ctx = h_ctx + g2_c * _expert_choice_ffn(_modulate(_rmsnorm(h_ctx, norm_ffn_g[l]), sh2_c, sc2_c),
                                                      router_w[l], exp_w_gate[l], exp_w_up[l], exp_w_down[l])
    return _rmsnorm(h_lat, final_norm_g)
```

```python
import functools

import numpy as np
import jax
import jax.numpy as jnp
from jax import lax
from jax.experimental import pallas as pl
from jax.experimental.pallas import tpu as pltpu

F32 = jnp.float32
BF16 = jnp.bfloat16
HIGHEST = lax.Precision.HIGHEST

D_MODEL = 1024
DEPTH = 2
GRID_W = 64
CTX_LEN = 256
RWKV_HEAD = 64
RWKV_WIDTH = 512
RWKV_HEADS = 8
DECAY_LORA = 64
ICLR_LORA = 64
GATE_LORA = 128
RWKV_COLS = 3 * RWKV_WIDTH + DECAY_LORA + ICLR_LORA + GATE_LORA
POOL_WIDTH = 256
POOL_WINDOWS = (2, 4, 8, 16)
POOL_GROUPS = 4
POOL_GC = 64
FOURIER_WIDTH = 256
FOURIER_HEADS = 4
FOURIER_HC = 64
IN_COLS = RWKV_COLS + POOL_WIDTH + FOURIER_WIDTH
MIX_SPLITS = (RWKV_COLS, RWKV_COLS + POOL_WIDTH)
N_EXPERTS = 16
CAPACITY_FACTOR = 2
N_MOD = 6
RMS_EPS = 1e-6
LNX_EPS = 64e-5

SUBLANES = 8
LANES = 128
BF16_ROWS = 16

SCAN_TC = 32
SCAN_SUBS = 8
TBLK = SCAN_TC * SCAN_SUBS
SCAN_NSEL = SCAN_TC // 2
SCAN_PARTS = 4
SCAN_KSEL = 2 * SCAN_PARTS * SCAN_TC
SCAN_NOPS = 5
SCAN_LROWS = SCAN_NOPS * RWKV_HEAD
KGROUPS = RWKV_HEAD // SUBLANES
HEAD_PAIRS = RWKV_HEADS // 2
assert TBLK == CTX_LEN and SCAN_PARTS * SCAN_TC == LANES


def _mirror_block(i, nblk):
    return jnp.where(i == 0, 0, nblk - i)


def _split3_exact(x):
    mask = jnp.uint32(0xFFFF0000)
    hi = pltpu.bitcast(pltpu.bitcast(x, jnp.uint32) & mask, F32)
    r1 = x - hi
    mid = pltpu.bitcast(pltpu.bitcast(r1, jnp.uint32) & mask, F32)
    return hi, mid, r1 - mid


def _sigmoid(x):
    return 1.0 / (1.0 + jnp.exp(-x))


def _softplus(x):
    return jnp.maximum(x, 0.0) + jnp.log(1.0 + jnp.exp(-jnp.abs(x)))


def _prep_kernel(p_ref, prev_ref, next_ref, sw_ref, kk_ref, ka_ref, rk_ref, w0_ref, a0_ref, dup_ref, aup_ref,
                 gup_ref, bd_ref, lt0_ref, lt1_ref, v_ref, gate_ref, bonus_ref):
    i = pl.program_id(1)
    nblk = pl.num_programs(1)
    prev_ok = i >= 2
    next_ok = jnp.logical_and(i >= 1, i <= nblk - 2)
    row = lax.broadcasted_iota(jnp.int32, (TBLK, 1), 0)

    def section(c0, c1):
        x = p_ref[0, :, c0:c1]
        prev_row = jnp.where(prev_ok, prev_ref[0, SUBLANES - 1:SUBLANES, c0:c1], 0.0)
        next_row = jnp.where(next_ok, next_ref[0, 0:1, c0:c1], 0.0)
        x_prev = jnp.where(row == 0, prev_row, pltpu.roll(x, 1, 0))
        x_next = jnp.where(row == TBLK - 1, next_row, pltpu.roll(x, TBLK - 1, 0))
        return x_prev * sw_ref[0:1, c0:c1] + x * sw_ref[1:2, c0:c1] + x_next * sw_ref[2:3, c0:c1]

    W = RWKV_WIDTH
    r = section(0, W)
    k = section(W, 2 * W)
    v = section(2 * W, 3 * W)
    xwa = section(3 * W, 3 * W + LANES)
    xg = section(3 * W + LANES, 3 * W + 2 * LANES)
    bd = bd_ref[...]

    kk = k * kk_ref[...]
    kk = kk * lax.rsqrt(jnp.dot(kk * kk, bd, precision=HIGHEST, preferred_element_type=F32) + 1e-12)
    v_ref[0] = v
    gate_ref[0] = jnp.dot(_sigmoid(xg), gup_ref[...], precision=HIGHEST, preferred_element_type=F32)
    txwa = jnp.tanh(xwa)
    a_neg = -kk

    def store_parts(lt_ref, d, o, x):
        parts = _split3_exact(x)
        for s in range(SCAN_SUBS):
            slot = s if d == 0 else SCAN_SUBS - 1 - s
            for p in range(3):
                lt_ref[0, slot, o, p * SCAN_TC:(p + 1) * SCAN_TC, :] = (
                    parts[p][s * SCAN_TC:(s + 1) * SCAN_TC, :].astype(BF16))
            lt_ref[0, slot, o, 3 * SCAN_TC:, :] = jnp.zeros((SCAN_TC, W), BF16)

    bonus = None
    for d, lt_ref in ((0, lt0_ref), (1, lt1_ref)):
        lw = jnp.dot(txwa, dup_ref[d], precision=HIGHEST, preferred_element_type=F32)
        w_log = -_softplus(-(w0_ref[d:d + 1, :] + lw)) - 0.5
        decay = jnp.exp(-jnp.exp(w_log))
        a_lr = _sigmoid(a0_ref[d:d + 1, :] + jnp.dot(xwa, aup_ref[d], precision=HIGHEST, preferred_element_type=F32))
        k_dir = k * (1.0 + (a_lr - 1.0) * ka_ref[...])
        bterm = jnp.dot(r * k_dir * rk_ref[...], bd, precision=HIGHEST, preferred_element_type=F32)
        bonus = bterm if bonus is None else bonus + bterm
        store_parts(lt_ref, d, 0, decay)
        store_parts(lt_ref, d, 1, a_neg)
        store_parts(lt_ref, d, 2, kk * a_lr)
        store_parts(lt_ref, d, 3, k_dir)
        store_parts(lt_ref, d, 4, r)
    bonus_ref[0] = bonus


def _rwkv_prep(p_all, shift_w, k_k, k_a, r_k, decay_w0, decay_up, iclr_a0, iclr_up, gate_up):
    B, TT, _ = p_all.shape
    nblk = TT // TBLK
    W = RWKV_WIDTH
    zeros = jnp.zeros((2, DECAY_LORA, W), F32)
    dup_pad = jnp.concatenate([decay_up, zeros], axis=1)
    aup_pad = jnp.concatenate([zeros, iclr_up], axis=1)
    bd = jnp.asarray(np.kron(np.eye(RWKV_HEADS, dtype=np.float32), np.ones((RWKV_HEAD, RWKV_HEAD), np.float32)))
    rows8 = TT // SUBLANES
    full = lambda shape: pl.BlockSpec(shape, lambda b, i: (0,) * len(shape))
    lt_shape = jax.ShapeDtypeStruct((B, nblk * SCAN_SUBS, SCAN_NOPS, LANES, W), BF16)
    tok_shape = jax.ShapeDtypeStruct((B, TT, W), F32)
    tok_spec = pl.BlockSpec((1, TBLK, W), lambda b, i: (b, i, 0))
    return pl.pallas_call(
        _prep_kernel,
        grid=(B, nblk),
        in_specs=[
            pl.BlockSpec((1, TBLK, RWKV_COLS), lambda b, i: (b, i, 0)),
            pl.BlockSpec((1, SUBLANES, RWKV_COLS), lambda b, i: (b, jnp.maximum(i * (TBLK // SUBLANES) - 1, 0), 0)),
            pl.BlockSpec((1, SUBLANES, RWKV_COLS),
                         lambda b, i: (b, jnp.minimum((i + 1) * (TBLK // SUBLANES), rows8 - 1), 0)),
            full((3, RWKV_COLS)), full((1, W)), full((1, W)), full((1, W)), full((2, W)), full((2, W)),
            full((2, LANES, W)), full((2, LANES, W)), full((GATE_LORA, W)), full((W, W)),
        ],
        out_specs=[
            pl.BlockSpec((1, SCAN_SUBS, SCAN_NOPS, LANES, W), lambda b, i: (b, i, 0, 0, 0)),
            pl.BlockSpec((1, SCAN_SUBS, SCAN_NOPS, LANES, W), lambda b, i: (b, _mirror_block(i, nblk), 0, 0, 0)),
            tok_spec, tok_spec, tok_spec,
        ],
        out_shape=[lt_shape, lt_shape, tok_shape, tok_shape, tok_shape],
        compiler_params=pltpu.CompilerParams(
            dimension_semantics=("arbitrary", "arbitrary"),
            vmem_limit_bytes=48 * 1024 * 1024,
        ),
        name="rwkv_prep",
    )(p_all, p_all, p_all, shift_w, k_k.reshape(1, W), k_a.reshape(1, W), r_k.reshape(1, W), decay_w0, iclr_a0,
      dup_pad, aup_pad, gate_up, bd)


def _sublane_allsum(x):
    x = x + pltpu.roll(x, 4, 0)
    x = x + pltpu.roll(x, 2, 0)
    return x + pltpu.roll(x, 1, 0)


def _tree_sum(xs):
    xs = list(xs)
    while len(xs) > 1:
        xs = [xs[i] + xs[i + 1] for i in range(0, len(xs), 2)]
    return xs[0]


def _scan_kernel(lt0_ref, lt1_ref, ltn0_ref, ltn1_ref, sel_ref, vf_ref, vb_ref, yf_ref, yb_ref,
                 e0_ref, e1_ref, l_ref, s_ref):
    i = pl.program_id(1)
    lt_refs = (lt0_ref, lt1_ref)
    ltn_refs = (ltn0_ref, ltn1_ref)
    v_refs = (vf_ref, vb_ref)
    y_refs = (yf_ref, yb_ref)

    def build_l(g, src_ref, idx):
        for o in range(SCAN_NOPS):
            xt = src_ref[0, idx, o].T
            l_ref[g, o * RWKV_HEAD:(o + 1) * RWKV_HEAD, 0:LANES] = xt[0:RWKV_HEAD]
            l_ref[g, o * RWKV_HEAD:(o + 1) * RWKV_HEAD, LANES:2 * LANES] = xt[RWKV_HEAD:2 * RWKV_HEAD]

    def expand(e_ref, n):
        for g in range(2):
            e_ref[g, n] = jnp.dot(l_ref[g], sel_ref[g, n], preferred_element_type=F32)

    @pl.when(i == 0)
    def _():
        s_ref[...] = jnp.zeros_like(s_ref)
        for g in range(2):
            build_l(g, lt_refs[g], 0)
        for n in range(SCAN_NSEL):
            expand(e0_ref, n)

    def scan_sub(sub, e_cur, e_nxt):
        @pl.when(sub < SCAN_SUBS - 1)
        def _():
            for g in range(2):
                build_l(g, lt_refs[g], sub + 1)

        @pl.when(sub == SCAN_SUBS - 1)
        def _():
            for g in range(2):
                build_l(g, ltn_refs[g], 0)

        state = [[s_ref[g, SUBLANES * j:SUBLANES * (j + 1), :] for j in range(KGROUPS)] for g in range(2)]
        for n in range(SCAN_NSEL):
            expand(e_nxt, n)
            for s in range(2):
                t = sub * SCAN_TC + 2 * n + s
                for g in range(2):
                    st = state[g]
                    trow = t if g == 0 else TBLK - 1 - t

                    def col(o, j, g=g, s=s, n=n):
                        r0 = o * RWKV_HEAD + SUBLANES * j
                        return e_cur[g, n, r0:r0 + SUBLANES, s * LANES:(s + 1) * LANES]

                    sa = _sublane_allsum(_tree_sum(st[j] * col(1, j) for j in range(KGROUPS)))
                    v8 = jnp.broadcast_to(v_refs[g][0, pl.ds(trow, 1), :], (SUBLANES, LANES))
                    nst = [st[j] * col(0, j) + col(2, j) * sa + col(3, j) * v8 for j in range(KGROUPS)]
                    y8 = _sublane_allsum(_tree_sum(nst[j] * col(4, j) for j in range(KGROUPS)))
                    y_refs[g][0, pl.ds(trow, 1), :] = y8[0:1, :]
                    state[g] = nst
        for g in range(2):
            for j in range(KGROUPS):
                s_ref[g, SUBLANES * j:SUBLANES * (j + 1), :] = state[g][j]

    def sub_pair(q, carry):
        scan_sub(2 * q, e0_ref, e1_ref)
        scan_sub(2 * q + 1, e1_ref, e0_ref)
        return carry

    lax.fori_loop(0, SCAN_SUBS // 2, sub_pair, 0)


def _scan_selector():
    sel = np.zeros((2, SCAN_NSEL, 2, SCAN_PARTS, SCAN_TC, 2, LANES), np.float32)
    for g in range(2):
        for n in range(SCAN_NSEL):
            for s in range(2):
                step = 2 * n + s if g == 0 else SCAN_TC - 1 - (2 * n + s)
                for j in range(2):
                    sel[g, n, j, :3, step, s, j * RWKV_HEAD:(j + 1) * RWKV_HEAD] = 1.0
    return jnp.asarray(sel.reshape(2, SCAN_NSEL, SCAN_KSEL, 2 * LANES), BF16)


def _delta_scan(lt0, lt1, v_all):
    B, TT, W = v_all.shape
    nblk = TT // TBLK
    nsub = nblk * SCAN_SUBS
    lt_spec = pl.BlockSpec((1, SCAN_SUBS, SCAN_NOPS, LANES, LANES), lambda c, i: (c // HEAD_PAIRS, i, 0, 0, c % HEAD_PAIRS))
    ltn_spec = pl.BlockSpec((1, 1, SCAN_NOPS, LANES, LANES),
                            lambda c, i: (c // HEAD_PAIRS, jnp.minimum((i + 1) * SCAN_SUBS, nsub - 1), 0, 0, c % HEAD_PAIRS))
    f_spec = pl.BlockSpec((1, TBLK, LANES), lambda c, i: (c // HEAD_PAIRS, i, c % HEAD_PAIRS))
    b_spec = pl.BlockSpec((1, TBLK, LANES), lambda c, i: (c // HEAD_PAIRS, _mirror_block(i, nblk), c % HEAD_PAIRS))
    y_shape = jax.ShapeDtypeStruct((B, TT, W), F32)
    return pl.pallas_call(
        _scan_kernel,
        grid=(B * HEAD_PAIRS, nblk),
        in_specs=[lt_spec, lt_spec, ltn_spec, ltn_spec,
                  pl.BlockSpec((2, SCAN_NSEL, SCAN_KSEL, 2 * LANES), lambda c, i: (0, 0, 0, 0)),
                  f_spec, b_spec],
        out_specs=[f_spec, b_spec],
        out_shape=[y_shape, y_shape],
        scratch_shapes=[
            pltpu.VMEM((2, SCAN_NSEL, SCAN_LROWS, 2 * LANES), F32),
            pltpu.VMEM((2, SCAN_NSEL, SCAN_LROWS, 2 * LANES), F32),
            pltpu.VMEM((2, SCAN_LROWS, SCAN_KSEL), BF16),
            pltpu.VMEM((2, RWKV_HEAD, LANES), F32),
        ],
        compiler_params=pltpu.CompilerParams(
            dimension_semantics=("arbitrary", "arbitrary"),
            vmem_limit_bytes=52 * 1024 * 1024,
        ),
        name="delta_scan",
    )(lt0, lt1, lt0, lt1, _scan_selector(), v_all, v_all)


def _heads(t):
    return t.reshape(t.shape[0], t.shape[1], RWKV_HEADS, RWKV_HEAD)


def _rmsnorm(x, g):
    y = x * lax.rsqrt(jnp.mean(x * x, axis=-1, keepdims=True) + RMS_EPS)
    return y * g


def _adaln(cond, ada_w, ada_b):
    return jnp.split(jax.nn.silu(cond) @ ada_w + ada_b, N_MOD, axis=-1)


def _modulate(x, shift, scale):
    return x * (1 + scale) + shift


def _rwkv_mixer(p_all, ctx_out, shift_w, decay_w0, decay_up, iclr_a0, iclr_up, gate_up, k_k, k_a, r_k, lnx_g, lnx_b):
    B, TT = p_all.shape[0], p_all.shape[1]
    lt0, lt1, v, gate, bonus = _rwkv_prep(p_all, shift_w[:, :RWKV_COLS], k_k, k_a, r_k, decay_w0, decay_up, iclr_a0,
                                          iclr_up, gate_up)
    y_f, y_b = _delta_scan(lt0, lt1, v)
    y = _heads(y_f + y_b)
    mu = jnp.mean(y, axis=-1, keepdims=True)
    var = jnp.mean(jnp.square(y - mu), axis=-1, keepdims=True)
    yn = ((y - mu) * lax.rsqrt(var + LNX_EPS)).reshape(B, TT, RWKV_WIDTH) * lnx_g + lnx_b
    out = (yn + bonus * v) * gate
    return out[:, CTX_LEN:], (out[:, :CTX_LEN] if ctx_out else None)


def _window_bounds(n, win):
    pos = jnp.arange(n)
    return jnp.maximum(pos - win // 2, 0), jnp.minimum(pos + win // 2, n)


def _pool2d_minus_self(u, rows):
    B, T, C = u.shape
    grid = u.reshape(B, rows, GRID_W, C)
    sat = jnp.pad(jnp.cumsum(jnp.cumsum(grid, axis=1), axis=2), ((0, 0), (1, 0), (1, 0), (0, 0)))
    means = []
    for gi, win in enumerate(POOL_WINDOWS):
        s = sat[..., gi * POOL_GC:(gi + 1) * POOL_GC]
        r_lo, r_hi = _window_bounds(rows, win)
        c_lo, c_hi = _window_bounds(GRID_W, win)
        s_hi, s_lo = s[:, r_hi], s[:, r_lo]
        box = s_hi[:, :, c_hi] - s_lo[:, :, c_hi] - s_hi[:, :, c_lo] + s_lo[:, :, c_lo]
        count = ((r_hi - r_lo)[:, None] * (c_hi - c_lo)[None, :]).astype(F32)
        means.append(box / count[None, :, :, None])
    pooled = jnp.concatenate(means, axis=-1).reshape(B, T, C)
    return pooled - u


def _pool1d_minus_self(u):
    B, T, C = u.shape
    cs = jnp.pad(jnp.cumsum(u, axis=1), ((0, 0), (1, 0), (0, 0)))
    means = []
    for gi, win in enumerate(POOL_WINDOWS):
        s = cs[..., gi * POOL_GC:(gi + 1) * POOL_GC]
        lo, hi = _window_bounds(T, win)
        means.append((s[:, hi] - s[:, lo]) / (hi - lo).astype(F32)[None, :, None])
    return jnp.concatenate(means, axis=-1) - u


def _pool_readout(p, pool_w, pool_scale):
    B, T = p.shape[0], p.shape[1]
    ph = p.reshape(B, T, POOL_GROUPS, POOL_GC)
    return jnp.einsum("btgc,gcd->btgd", ph, pool_w).reshape(B, T, POOL_WIDTH) * pool_scale


def _fourier_mixer(u, fourier_w):
    B, T = u.shape[0], u.shape[1]
    uh = u.reshape(B, T, FOURIER_HEADS, FOURIER_HC)
    f = jnp.real(jnp.fft.fft2(uh, axes=(1, 3), norm="ortho"))
    return jnp.einsum("bthc,hcd->bthd", f, fourier_w).reshape(B, T, FOURIER_WIDTH)


def _token_mixing(xn_lat, xn_ctx, rows, ctx_out, w_in, shift_w, decay_w0, decay_up, iclr_a0, iclr_up, gate_up,
                  k_k, k_a, r_k, lnx_g, lnx_b, pool_w, pool_scale, fourier_w, w_out):
    p_lat = xn_lat @ w_in
    p_ctx = xn_ctx @ w_in
    p_all = jnp.concatenate([p_ctx, p_lat], axis=1)
    rw_l, rw_c = _rwkv_mixer(p_all, ctx_out, shift_w, decay_w0, decay_up, iclr_a0, iclr_up, gate_up, k_k, k_a, r_k,
                             lnx_g, lnx_b)
    _, u_pool_l, u_four_l = jnp.split(p_lat, MIX_SPLITS, axis=-1)
    out_lat = jnp.concatenate([rw_l, _pool_readout(_pool2d_minus_self(u_pool_l, rows), pool_w, pool_scale),
                               _fourier_mixer(u_four_l, fourier_w)], axis=-1) @ w_out
    if not ctx_out:
        return out_lat, None
    _, u_pool_c, u_four_c = jnp.split(p_ctx, MIX_SPLITS, axis=-1)
    out_ctx = jnp.concatenate([rw_c, _pool_readout(_pool1d_minus_self(u_pool_c), pool_w, pool_scale),
                               _fourier_mixer(u_four_c, fourier_w)], axis=-1) @ w_out
    return out_lat, out_ctx


MOE_TB = 256
MOE_ALIGN = BF16_ROWS
MOE_WIN = MOE_TB + MOE_ALIGN
MOE_EG = 4
FFN_ROWS = 512


def _ffn_pre_kernel(h_ref, g_ref, sh_ref, sc_ref, rw_ref, rwt_ref, hn_ref, aff_ref, afft_ref):
    x = h_ref[0]
    xn = x * lax.rsqrt(jnp.mean(x * x, axis=-1, keepdims=True) + RMS_EPS) * g_ref[...]
    hn = xn * (1.0 + sc_ref[0]) + sh_ref[0]
    hn_ref[0] = hn.astype(BF16)
    logits = jnp.dot(hn, rw_ref[...], precision=HIGHEST, preferred_element_type=F32)
    ex = jnp.exp(logits - jnp.max(logits, axis=-1, keepdims=True))
    aff_ref[0] = ex / jnp.sum(ex, axis=-1, keepdims=True)
    logits_t = lax.dot_general(rwt_ref[...], hn, (((1,), (1,)), ((), ())), precision=HIGHEST,
                               preferred_element_type=F32)
    ext = jnp.exp(logits_t - jnp.max(logits_t, axis=0, keepdims=True))
    afft_ref[0] = ext / jnp.sum(ext, axis=0, keepdims=True)


def _ffn_pre(h, g, shift, scale, router_w):
    B, T, D = h.shape
    E = N_EXPERTS
    tok = lambda w: pl.BlockSpec((1, MOE_TB, w), lambda b, j: (b, j, 0))
    vec = pl.BlockSpec((1, 1, D), lambda b, j: (b, 0, 0))
    return pl.pallas_call(
        _ffn_pre_kernel,
        grid=(B, T // MOE_TB),
        in_specs=[tok(D), pl.BlockSpec((1, D), lambda b, j: (0, 0)), vec, vec,
                  pl.BlockSpec((D, E), lambda b, j: (0, 0)), pl.BlockSpec((E, D), lambda b, j: (0, 0))],
        out_specs=[tok(D), tok(E), pl.BlockSpec((1, E, MOE_TB), lambda b, j: (b, 0, j))],
        out_shape=[jax.ShapeDtypeStruct((B, T, D), BF16), jax.ShapeDtypeStruct((B, T, E), F32),
                   jax.ShapeDtypeStruct((B, E, T), F32)],
        compiler_params=pltpu.CompilerParams(dimension_semantics=("arbitrary", "arbitrary")),
        name="ffn_pre",
    )(h, g.reshape(1, D), shift, scale, router_w, router_w.T)


def _topk_kernel(cap, afft_ref, tri_ref, ones_ref, pos_ref, start_ref):
    x = afft_ref[0]
    E, T = x.shape
    xi = pltpu.bitcast(x, jnp.int32)
    capf = jnp.float32(cap)

    def count(mask):
        return jnp.sum(jnp.where(mask, 1.0, 0.0), axis=1, keepdims=True)

    thr = jnp.zeros((E, 1), jnp.int32)
    for bit in range(30, -1, -1):
        cand = thr | jnp.int32(1 << bit)
        thr = jnp.where(count(xi >= cand) >= capf, cand, thr)
    need = capf - count(xi > thr)
    tri = tri_ref[...]
    ones = ones_ref[...]
    eq_carry = jnp.zeros((E, MOE_TB), F32)
    sel_carry = jnp.zeros((E, MOE_TB), F32)
    for c in range(T // MOE_TB):
        xc = xi[:, c * MOE_TB:(c + 1) * MOE_TB]
        eq = jnp.where(xc == thr, 1.0, 0.0)
        eqb = eq.astype(BF16)
        eq_rank = eq_carry + jnp.dot(eqb, tri, preferred_element_type=F32) - eq
        eq_carry = eq_carry + jnp.dot(eqb, ones, preferred_element_type=F32)
        sel = jnp.where(jnp.logical_or(xc > thr, jnp.logical_and(xc == thr, eq_rank < need)), 1.0, 0.0)
        selb = sel.astype(BF16)
        slot = sel_carry + jnp.dot(selb, tri, preferred_element_type=F32) - sel
        pos_ref[0, :, c * MOE_TB:(c + 1) * MOE_TB] = jnp.where(sel > 0.0, slot, -1.0).astype(jnp.int32)
        start_ref[0, :, c:c + 1] = sel_carry[:, 0:1].astype(jnp.int32)
        sel_carry = sel_carry + jnp.dot(selb, ones, preferred_element_type=F32)


def _topk_select(afft, cap):
    B, E, T = afft.shape
    nb = T // MOE_TB
    tri = jnp.asarray(np.triu(np.ones((MOE_TB, MOE_TB), np.float32)), BF16)
    ones = jnp.ones((MOE_TB, MOE_TB), BF16)
    sq = pl.BlockSpec((MOE_TB, MOE_TB), lambda b: (0, 0))
    return pl.pallas_call(
        functools.partial(_topk_kernel, cap),
        grid=(B,),
        in_specs=[pl.BlockSpec((1, E, T), lambda b: (b, 0, 0)), sq, sq],
        out_specs=[pl.BlockSpec((1, E, T), lambda b: (b, 0, 0)), pl.BlockSpec((1, E, nb), lambda b: (b, 0, 0))],
        out_shape=[jax.ShapeDtypeStruct((B, E, T), jnp.int32), jax.ShapeDtypeStruct((B, E, nb), jnp.int32)],
        compiler_params=pltpu.CompilerParams(dimension_semantics=("arbitrary",),
                                             vmem_limit_bytes=48 * 1024 * 1024),
        name="topk_select",
    )(afft, tri, ones)


def _slot_onehot(pos_row, start):
    base = pl.multiple_of((start // MOE_ALIGN) * MOE_ALIGN, MOE_ALIGN)
    s_iota = lax.broadcasted_iota(jnp.int32, (MOE_WIN, MOE_TB), 0)
    onehot = jnp.where(pos_row - base == s_iota, 1.0, 0.0).astype(BF16)
    return onehot, base


def _expert_kernel(cap, start_ref, hn_ref, pos_ref, wg_ref, wu_ref, wd_ref, ye_ref, xe_ref):
    b, e, j = pl.program_id(0), pl.program_id(1), pl.program_id(2)
    nb = pl.num_programs(2)

    @pl.when(j == 0)
    def _():
        xe_ref[...] = jnp.zeros_like(xe_ref)

    onehot, base = _slot_onehot(pos_ref[0, 0], start_ref[(b * N_EXPERTS + e) * nb + j])
    xe_ref[pl.ds(base, MOE_WIN), :] += jnp.dot(onehot, hn_ref[0], preferred_element_type=F32)

    @pl.when(j == nb - 1)
    def _():
        rows = min(FFN_ROWS, cap)
        for r0 in range(0, cap, rows):
            xb = xe_ref[r0:r0 + rows, :].astype(BF16)
            gt = jnp.dot(xb, wg_ref[0], preferred_element_type=F32)
            up = jnp.dot(xb, wu_ref[0], preferred_element_type=F32)
            hid = (gt * _sigmoid(gt) * up).astype(BF16)
            ye_ref[0, 0, r0:r0 + rows, :] = jnp.dot(hid, wd_ref[0], preferred_element_type=F32).astype(BF16)
        ye_ref[0, 0, cap:, :] = jnp.zeros((MOE_WIN, ye_ref.shape[-1]), BF16)


def _expert_ffn(hn, pos4, start_flat, wg, wu, wd, cap):
    B, T, D = hn.shape
    E = N_EXPERTS
    capp = cap + MOE_WIN
    wspec = pl.BlockSpec((1, D, D), lambda b, e, j, s: (e, 0, 0))
    return pl.pallas_call(
        functools.partial(_expert_kernel, cap),
        grid_spec=pltpu.PrefetchScalarGridSpec(
            num_scalar_prefetch=1,
            grid=(B, E, T // MOE_TB),
            in_specs=[pl.BlockSpec((1, MOE_TB, D), lambda b, e, j, s: (b, j, 0)),
                      pl.BlockSpec((1, 1, 1, MOE_TB), lambda b, e, j, s: (b, e, 0, j)),
                      wspec, wspec, wspec],
            out_specs=pl.BlockSpec((1, 1, capp, D), lambda b, e, j, s: (b, e, 0, 0)),
            scratch_shapes=[pltpu.VMEM((capp, D), F32)],
        ),
        out_shape=jax.ShapeDtypeStruct((B, E, capp, D), BF16),
        compiler_params=pltpu.CompilerParams(dimension_semantics=("arbitrary", "arbitrary", "arbitrary"),
                                             vmem_limit_bytes=56 * 1024 * 1024),
        name="expert_ffn",
    )(start_flat, hn, pos4, wg, wu, wd)


def _combine_kernel(eg, start_ref, ye_ref, pos_ref, aff_ref, g2_ref, h_ref, out_ref):
    b, g, j = pl.program_id(0), pl.program_id(1), pl.program_id(2)
    nb = pl.num_programs(2)
    aff = aff_ref[0]
    lane = lax.broadcasted_iota(jnp.int32, aff.shape, 1)
    acc = jnp.zeros(h_ref.shape[1:], F32)
    for el in range(eg):
        e = g * eg + el
        onehot, base = _slot_onehot(pos_ref[0, el], start_ref[(b * N_EXPERTS + e) * nb + j])
        got = lax.dot_general(onehot, ye_ref[0, el, pl.ds(base, MOE_WIN), :], (((0,), (0,)), ((), ())),
                              preferred_element_type=F32)
        gate = jnp.sum(jnp.where(lane == e, aff, 0.0), axis=-1, keepdims=True)
        acc = acc + got * gate
    out_ref[0] = h_ref[0] + g2_ref[0] * acc


def _moe_combine(h, ye, pos4, aff, g2, start_flat):
    B, T, D = h.shape
    E = N_EXPERTS
    capp = ye.shape[2]
    nb = T // MOE_TB
    eg = MOE_EG if nb > 1 else E
    tok = lambda w: pl.BlockSpec((1, MOE_TB, w), lambda b, g, j, s: (b, j, 0))
    return pl.pallas_call(
        functools.partial(_combine_kernel, eg),
        grid_spec=pltpu.PrefetchScalarGridSpec(
            num_scalar_prefetch=1,
            grid=(B, E // eg, nb),
            in_specs=[pl.BlockSpec((1, eg, capp, D), lambda b, g, j, s: (b, g, 0, 0),
                                   pipeline_mode=pl.Buffered(1)),
                      pl.BlockSpec((1, eg, 1, MOE_TB), lambda b, g, j, s: (b, g, 0, j)),
                      tok(E), pl.BlockSpec((1, 1, D), lambda b, g, j, s: (b, 0, 0)), tok(D)],
            out_specs=tok(D),
        ),
        out_shape=jax.ShapeDtypeStruct((B, T, D), F32),
        input_output_aliases={5: 0},
        compiler_params=pltpu.CompilerParams(dimension_semantics=("arbitrary", "arbitrary", "arbitrary"),
                                             vmem_limit_bytes=56 * 1024 * 1024),
        name="moe_combine",
    )(start_flat, ye, pos4, aff, g2, h)


def _moe_block(h, norm_g, shift, scale, g2, router_w, wg, wu, wd):
    B, T, D = h.shape
    cap = CAPACITY_FACTOR * T // N_EXPERTS
    hn, aff, afft = _ffn_pre(h, norm_g, shift, scale, router_w)
    pos, start = _topk_select(afft, cap)
    pos4 = pos.reshape(B, N_EXPERTS, 1, T)
    start_flat = start.reshape(-1)
    ye = _expert_ffn(hn, pos4, start_flat, wg, wu, wd, cap)
    return _moe_combine(h, ye, pos4, aff, g2, start_flat)


def kernel(x, c, ctx, c_ctx, ada_w, ada_b, norm_mix_g, norm_ffn_g, w_in, shift_w, decay_w0, decay_up, iclr_a0,
           iclr_up, gate_up, k_k, k_a, r_k, lnx_g, lnx_b, pool_w, pool_scale, fourier_w, w_out, router_w,
           exp_w_gate, exp_w_up, exp_w_down, final_norm_g):
    rows = x.shape[1] // GRID_W
    h_lat, h_ctx = x, ctx
    for l in range(DEPTH):
        ctx_out = l < DEPTH - 1
        sh1_l, sc1_l, g1_l, sh2_l, sc2_l, g2_l = [m[:, None, :] for m in _adaln(c, ada_w[l], ada_b[l])]
        sh1_c, sc1_c, g1_c, sh2_c, sc2_c, g2_c = _adaln(c_ctx, ada_w[l], ada_b[l])
        mix_lat, mix_ctx = _token_mixing(
            _modulate(_rmsnorm(h_lat, norm_mix_g[l]), sh1_l, sc1_l),
            _modulate(_rmsnorm(h_ctx, norm_mix_g[l]), sh1_c, sc1_c),
            rows, ctx_out, w_in[l], shift_w[l], decay_w0[l], decay_up[l], iclr_a0[l], iclr_up[l], gate_up[l],
            k_k[l], k_a[l], r_k[l], lnx_g[l], lnx_b[l], pool_w[l], pool_scale[l], fourier_w[l], w_out[l])
        B, D = h_lat.shape[0], h_lat.shape[2]
        experts = (exp_w_gate[l].astype(BF16), exp_w_up[l].astype(BF16), exp_w_down[l].astype(BF16))
        h_lat = h_lat + g1_l * mix_lat
        h_lat = _moe_block(h_lat, norm_ffn_g[l], sh2_l, sc2_l, g2_l, router_w[l], *experts)
        if ctx_out:
            per_sample = lambda m: jnp.broadcast_to(m[None, None, :], (B, 1, D))
            h_ctx = h_ctx + g1_c * mix_ctx
            h_ctx = _moe_block(h_ctx, norm_ffn_g[l], per_sample(sh2_c), per_sample(sc2_c), per_sample(g2_c),
                               router_w[l], *experts)
    return _rmsnorm(h_lat, final_norm_g)
```

```python
import functools

import numpy as np
import jax
import jax.numpy as jnp
from jax import lax
from jax.experimental import pallas as pl
from jax.experimental.pallas import tpu as pltpu

F32 = jnp.float32
BF16 = jnp.bfloat16
HIGHEST = lax.Precision.HIGHEST

D_MODEL = 1024
DEPTH = 2
GRID_W = 64
CTX_LEN = 256
RWKV_HEAD = 64
RWKV_WIDTH = 512
RWKV_HEADS = 8
DECAY_LORA = 64
ICLR_LORA = 64
GATE_LORA = 128
RWKV_COLS = 3 * RWKV_WIDTH + DECAY_LORA + ICLR_LORA + GATE_LORA
POOL_WIDTH = 256
POOL_WINDOWS = (2, 4, 8, 16)
POOL_GROUPS = 4
POOL_GC = 64
FOURIER_WIDTH = 256
FOURIER_HEADS = 4
FOURIER_HC = 64
IN_COLS = RWKV_COLS + POOL_WIDTH + FOURIER_WIDTH
MIX_SPLITS = (RWKV_COLS, RWKV_COLS + POOL_WIDTH)
N_EXPERTS = 16
CAPACITY_FACTOR = 2
N_MOD = 6
RMS_EPS = 1e-6
LNX_EPS = 64e-5

SUBLANES = 8
LANES = 128
BF16_ROWS = 16

SCAN_TC = 32
SCAN_SUBS = 8
TBLK = SCAN_TC * SCAN_SUBS
SCAN_NSEL = SCAN_TC // 2
SCAN_PARTS = 4
SCAN_KSEL = 2 * SCAN_PARTS * SCAN_TC
SCAN_NOPS = 5
SCAN_LROWS = SCAN_NOPS * RWKV_HEAD
KGROUPS = RWKV_HEAD // SUBLANES
HEAD_PAIRS = RWKV_HEADS // 2
assert TBLK == CTX_LEN and SCAN_PARTS * SCAN_TC == LANES


def _mirror_block(i, nblk):
    return jnp.where(i == 0, 0, nblk - i)


def _split3_exact(x):
    mask = jnp.uint32(0xFFFF0000)
    hi = pltpu.bitcast(pltpu.bitcast(x, jnp.uint32) & mask, F32)
    r1 = x - hi
    mid = pltpu.bitcast(pltpu.bitcast(r1, jnp.uint32) & mask, F32)
    return hi, mid, r1 - mid


def _sigmoid(x):
    return 1.0 / (1.0 + jnp.exp(-x))


def _softplus(x):
    return jnp.maximum(x, 0.0) + jnp.log(1.0 + jnp.exp(-jnp.abs(x)))


def _prep_kernel(p_ref, prev_ref, next_ref, sw_ref, kk_ref, ka_ref, rk_ref, w0_ref, a0_ref, dup_ref, aup_ref,
                 gup_ref, bd_ref, lt0_ref, lt1_ref, v_ref, gate_ref, bonus_ref):
    i = pl.program_id(1)
    nblk = pl.num_programs(1)
    prev_ok = i >= 2
    next_ok = jnp.logical_and(i >= 1, i <= nblk - 2)
    row = lax.broadcasted_iota(jnp.int32, (TBLK, 1), 0)

    def section(c0, c1):
        x = p_ref[0, :, c0:c1]
        prev_row = jnp.where(prev_ok, prev_ref[0, SUBLANES - 1:SUBLANES, c0:c1], 0.0)
        next_row = jnp.where(next_ok, next_ref[0, 0:1, c0:c1], 0.0)
        x_prev = jnp.where(row == 0, prev_row, pltpu.roll(x, 1, 0))
        x_next = jnp.where(row == TBLK - 1, next_row, pltpu.roll(x, TBLK - 1, 0))
        return x_prev * sw_ref[0:1, c0:c1] + x * sw_ref[1:2, c0:c1] + x_next * sw_ref[2:3, c0:c1]

    W = RWKV_WIDTH
    r = section(0, W)
    k = section(W, 2 * W)
    v = section(2 * W, 3 * W)
    xwa = section(3 * W, 3 * W + LANES)
    xg = section(3 * W + LANES, 3 * W + 2 * LANES)
    bd = bd_ref[...]

    kk = k * kk_ref[...]
    kk = kk * lax.rsqrt(jnp.dot(kk * kk, bd, precision=HIGHEST, preferred_element_type=F32) + 1e-12)
    v_ref[0] = v
    gate_ref[0] = jnp.dot(_sigmoid(xg), gup_ref[...], precision=HIGHEST, preferred_element_type=F32)
    txwa = jnp.tanh(xwa)
    a_neg = -kk

    def store_parts(lt_ref, d, o, x):
        parts = _split3_exact(x)
        for s in range(SCAN_SUBS):
            slot = s if d == 0 else SCAN_SUBS - 1 - s
            for p in range(3):
                lt_ref[0, slot, o, p * SCAN_TC:(p + 1) * SCAN_TC, :] = (
                    parts[p][s * SCAN_TC:(s + 1) * SCAN_TC, :].astype(BF16))
            lt_ref[0, slot, o, 3 * SCAN_TC:, :] = jnp.zeros((SCAN_TC, W), BF16)

    bonus = None
    for d, lt_ref in ((0, lt0_ref), (1, lt1_ref)):
        lw = jnp.dot(txwa, dup_ref[d], precision=HIGHEST, preferred_element_type=F32)
        w_log = -_softplus(-(w0_ref[d:d + 1, :] + lw)) - 0.5
        decay = jnp.exp(-jnp.exp(w_log))
        a_lr = _sigmoid(a0_ref[d:d + 1, :] + jnp.dot(xwa, aup_ref[d], precision=HIGHEST, preferred_element_type=F32))
        k_dir = k * (1.0 + (a_lr - 1.0) * ka_ref[...])
        bterm = jnp.dot(r * k_dir * rk_ref[...], bd, precision=HIGHEST, preferred_element_type=F32)
        bonus = bterm if bonus is None else bonus + bterm
        store_parts(lt_ref, d, 0, decay)
        store_parts(lt_ref, d, 1, a_neg)
        store_parts(lt_ref, d, 2, kk * a_lr)
        store_parts(lt_ref, d, 3, k_dir)
        store_parts(lt_ref, d, 4, r)
    bonus_ref[0] = bonus


def _rwkv_prep(p_all, shift_w, k_k, k_a, r_k, decay_w0, decay_up, iclr_a0, iclr_up, gate_up):
    B, TT, _ = p_all.shape
    nblk = TT // TBLK
    W = RWKV_WIDTH
    zeros = jnp.zeros((2, DECAY_LORA, W), F32)
    dup_pad = jnp.concatenate([decay_up, zeros], axis=1)
    aup_pad = jnp.concatenate([zeros, iclr_up], axis=1)
    bd = jnp.asarray(np.kron(np.eye(RWKV_HEADS, dtype=np.float32), np.ones((RWKV_HEAD, RWKV_HEAD), np.float32)))
    rows8 = TT // SUBLANES
    full = lambda shape: pl.BlockSpec(shape, lambda b, i: (0,) * len(shape))
    lt_shape = jax.ShapeDtypeStruct((B, nblk * SCAN_SUBS, SCAN_NOPS, LANES, W), BF16)
    tok_shape = jax.ShapeDtypeStruct((B, TT, W), F32)
    tok_spec = pl.BlockSpec((1, TBLK, W), lambda b, i: (b, i, 0))
    return pl.pallas_call(
        _prep_kernel,
        grid=(B, nblk),
        in_specs=[
            pl.BlockSpec((1, TBLK, RWKV_COLS), lambda b, i: (b, i, 0)),
            pl.BlockSpec((1, SUBLANES, RWKV_COLS), lambda b, i: (b, jnp.maximum(i * (TBLK // SUBLANES) - 1, 0), 0)),
            pl.BlockSpec((1, SUBLANES, RWKV_COLS),
                         lambda b, i: (b, jnp.minimum((i + 1) * (TBLK // SUBLANES), rows8 - 1), 0)),
            full((3, RWKV_COLS)), full((1, W)), full((1, W)), full((1, W)), full((2, W)), full((2, W)),
            full((2, LANES, W)), full((2, LANES, W)), full((GATE_LORA, W)), full((W, W)),
        ],
        out_specs=[
            pl.BlockSpec((1, SCAN_SUBS, SCAN_NOPS, LANES, W), lambda b, i: (b, i, 0, 0, 0)),
            pl.BlockSpec((1, SCAN_SUBS, SCAN_NOPS, LANES, W), lambda b, i: (b, _mirror_block(i, nblk), 0, 0, 0)),
            tok_spec, tok_spec, tok_spec,
        ],
        out_shape=[lt_shape, lt_shape, tok_shape, tok_shape, tok_shape],
        compiler_params=pltpu.CompilerParams(
            dimension_semantics=("arbitrary", "arbitrary"),
            vmem_limit_bytes=48 * 1024 * 1024,
        ),
        name="rwkv_prep",
    )(p_all, p_all, p_all, shift_w, k_k.reshape(1, W), k_a.reshape(1, W), r_k.reshape(1, W), decay_w0, iclr_a0,
      dup_pad, aup_pad, gate_up, bd)


def _sublane_allsum(x):
    x = x + pltpu.roll(x, 4, 0)
    x = x + pltpu.roll(x, 2, 0)
    return x + pltpu.roll(x, 1, 0)


def _tree_sum(xs):
    xs = list(xs)
    while len(xs) > 1:
        xs = [xs[i] + xs[i + 1] for i in range(0, len(xs), 2)]
    return xs[0]


def _scan_kernel(lt0_ref, lt1_ref, ltn0_ref, ltn1_ref, sel_ref, vf_ref, vb_ref, yf_ref, yb_ref,
                 e0_ref, e1_ref, l_ref, s_ref):
    i = pl.program_id(1)
    lt_refs = (lt0_ref, lt1_ref)
    ltn_refs = (ltn0_ref, ltn1_ref)
    v_refs = (vf_ref, vb_ref)
    y_refs = (yf_ref, yb_ref)

    def build_l(g, src_ref, idx):
        for o in range(SCAN_NOPS):
            xt = src_ref[0, idx, o].T
            l_ref[g, o * RWKV_HEAD:(o + 1) * RWKV_HEAD, 0:LANES] = xt[0:RWKV_HEAD]
            l_ref[g, o * RWKV_HEAD:(o + 1) * RWKV_HEAD, LANES:2 * LANES] = xt[RWKV_HEAD:2 * RWKV_HEAD]

    def expand(e_ref, n):
        for g in range(2):
            e_ref[g, n] = jnp.dot(l_ref[g], sel_ref[g, n], preferred_element_type=F32)

    @pl.when(i == 0)
    def _():
        s_ref[...] = jnp.zeros_like(s_ref)
        for g in range(2):
            build_l(g, lt_refs[g], 0)
        for n in range(SCAN_NSEL):
            expand(e0_ref, n)

    def scan_sub(sub, e_cur, e_nxt):
        @pl.when(sub < SCAN_SUBS - 1)
        def _():
            for g in range(2):
                build_l(g, lt_refs[g], sub + 1)

        @pl.when(sub == SCAN_SUBS - 1)
        def _():
            for g in range(2):
                build_l(g, ltn_refs[g], 0)

        state = [[s_ref[g, SUBLANES * j:SUBLANES * (j + 1), :] for j in range(KGROUPS)] for g in range(2)]
        for n in range(SCAN_NSEL):
            expand(e_nxt, n)
            for s in range(2):
                t = sub * SCAN_TC + 2 * n + s
                for g in range(2):
                    st = state[g]
                    trow = t if g == 0 else TBLK - 1 - t

                    def col(o, j, g=g, s=s, n=n):
                        r0 = o * RWKV_HEAD + SUBLANES * j
                        return e_cur[g, n, r0:r0 + SUBLANES, s * LANES:(s + 1) * LANES]

                    sa = _sublane_allsum(_tree_sum(st[j] * col(1, j) for j in range(KGROUPS)))
                    v8 = jnp.broadcast_to(v_refs[g][0, pl.ds(trow, 1), :], (SUBLANES, LANES))
                    nst = [st[j] * col(0, j) + col(2, j) * sa + col(3, j) * v8 for j in range(KGROUPS)]
                    y8 = _sublane_allsum(_tree_sum(nst[j] * col(4, j) for j in range(KGROUPS)))
                    y_refs[g][0, pl.ds(trow, 1), :] = y8[0:1, :]
                    state[g] = nst
        for g in range(2):
            for j in range(KGROUPS):
                s_ref[g, SUBLANES * j:SUBLANES * (j + 1), :] = state[g][j]

    def sub_pair(q, carry):
        scan_sub(2 * q, e0_ref, e1_ref)
        scan_sub(2 * q + 1, e1_ref, e0_ref)
        return carry

    lax.fori_loop(0, SCAN_SUBS // 2, sub_pair, 0)


def _scan_selector():
    sel = np.zeros((2, SCAN_NSEL, 2, SCAN_PARTS, SCAN_TC, 2, LANES), np.float32)
    for g in range(2):
        for n in range(SCAN_NSEL):
            for s in range(2):
                step = 2 * n + s if g == 0 else SCAN_TC - 1 - (2 * n + s)
                for j in range(2):
                    sel[g, n, j, :3, step, s, j * RWKV_HEAD:(j + 1) * RWKV_HEAD] = 1.0
    return jnp.asarray(sel.reshape(2, SCAN_NSEL, SCAN_KSEL, 2 * LANES), BF16)


def _delta_scan(lt0, lt1, v_all):
    B, TT, W = v_all.shape
    nblk = TT // TBLK
    nsub = nblk * SCAN_SUBS
    lt_spec = pl.BlockSpec((1, SCAN_SUBS, SCAN_NOPS, LANES, LANES), lambda c, i: (c // HEAD_PAIRS, i, 0, 0, c % HEAD_PAIRS))
    ltn_spec = pl.BlockSpec((1, 1, SCAN_NOPS, LANES, LANES),
                            lambda c, i: (c // HEAD_PAIRS, jnp.minimum((i + 1) * SCAN_SUBS, nsub - 1), 0, 0, c % HEAD_PAIRS))
    f_spec = pl.BlockSpec((1, TBLK, LANES), lambda c, i: (c // HEAD_PAIRS, i, c % HEAD_PAIRS))
    b_spec = pl.BlockSpec((1, TBLK, LANES), lambda c, i: (c // HEAD_PAIRS, _mirror_block(i, nblk), c % HEAD_PAIRS))
    y_shape = jax.ShapeDtypeStruct((B, TT, W), F32)
    return pl.pallas_call(
        _scan_kernel,
        grid=(B * HEAD_PAIRS, nblk),
        in_specs=[lt_spec, lt_spec, ltn_spec, ltn_spec,
                  pl.BlockSpec((2, SCAN_NSEL, SCAN_KSEL, 2 * LANES), lambda c, i: (0, 0, 0, 0)),
                  f_spec, b_spec],
        out_specs=[f_spec, b_spec],
        out_shape=[y_shape, y_shape],
        scratch_shapes=[
            pltpu.VMEM((2, SCAN_NSEL, SCAN_LROWS, 2 * LANES), F32),
            pltpu.VMEM((2, SCAN_NSEL, SCAN_LROWS, 2 * LANES), F32),
            pltpu.VMEM((2, SCAN_LROWS, SCAN_KSEL), BF16),
            pltpu.VMEM((2, RWKV_HEAD, LANES), F32),
        ],
        compiler_params=pltpu.CompilerParams(
            dimension_semantics=("arbitrary", "arbitrary"),
            vmem_limit_bytes=52 * 1024 * 1024,
        ),
        name="delta_scan",
    )(lt0, lt1, lt0, lt1, _scan_selector(), v_all, v_all)


def _heads(t):
    return t.reshape(t.shape[0], t.shape[1], RWKV_HEADS, RWKV_HEAD)


def _rmsnorm(x, g):
    y = x * lax.rsqrt(jnp.mean(x * x, axis=-1, keepdims=True) + RMS_EPS)
    return y * g


def _adaln(cond, ada_w, ada_b):
    return jnp.split(jax.nn.silu(cond) @ ada_w + ada_b, N_MOD, axis=-1)


def _modulate(x, shift, scale):
    return x * (1 + scale) + shift


def _mix_in_kernel(h_ref, g_ref, sh_ref, sc_ref, w_ref, *rest):
    prw_ref, ppool_ref, pfour_ref = rest[-3:]
    x = h_ref[0]
    xn = x * lax.rsqrt(jnp.mean(x * x, axis=-1, keepdims=True) + RMS_EPS) * g_ref[...]
    hn = (xn * (1.0 + sc_ref[0]) + sh_ref[0]).astype(BF16)
    p = jnp.dot(hn, w_ref[...], preferred_element_type=F32)
    prw_ref[0] = p[:, :RWKV_COLS]
    ppool_ref[0] = p[:, RWKV_COLS:RWKV_COLS + POOL_WIDTH]
    pfour_ref[0] = p[:, RWKV_COLS + POOL_WIDTH:]


def _mix_in(h, g, shift, scale, w_in_bf16, blk_off, ttot, prw_all=None):
    B, T, D = h.shape
    tok = lambda w: pl.BlockSpec((1, TBLK, w), lambda b, j: (b, j, 0))
    vec = pl.BlockSpec((1, 1, D), lambda b, j: (b, 0, 0))
    in_specs = [tok(D), pl.BlockSpec((1, D), lambda b, j: (0, 0)), vec, vec,
                pl.BlockSpec((D, IN_COLS), lambda b, j: (0, 0))]
    args = [h, g.reshape(1, D), shift, scale, w_in_bf16]
    aliases = {}
    if prw_all is not None:
        in_specs.append(pl.BlockSpec(memory_space=pl.ANY))
        args.append(prw_all)
        aliases = {5: 0}
    return pl.pallas_call(
        _mix_in_kernel,
        grid=(B, T // TBLK),
        in_specs=in_specs,
        out_specs=[pl.BlockSpec((1, TBLK, RWKV_COLS), lambda b, j: (b, j + blk_off, 0)),
                   tok(POOL_WIDTH), tok(FOURIER_WIDTH)],
        out_shape=[jax.ShapeDtypeStruct((B, ttot, RWKV_COLS), F32), jax.ShapeDtypeStruct((B, T, POOL_WIDTH), F32),
                   jax.ShapeDtypeStruct((B, T, FOURIER_WIDTH), F32)],
        input_output_aliases=aliases,
        compiler_params=pltpu.CompilerParams(dimension_semantics=("arbitrary", "arbitrary"),
                                             vmem_limit_bytes=48 * 1024 * 1024),
        name="mix_in",
    )(*args)


def _mix_out_kernel(yf_ref, yb_ref, v_ref, bonus_ref, gate_ref, pool_ref, four_ref, h_ref, lg_ref, lb_ref, bd_ref,
                    wout_ref, g1_ref, out_ref):
    y = yf_ref[0] + yb_ref[0]
    bd = bd_ref[...]
    inv_n = 1.0 / RWKV_HEAD
    mu = jnp.dot(y, bd, precision=HIGHEST, preferred_element_type=F32) * inv_n
    dev = y - mu
    var = jnp.dot(dev * dev, bd, precision=HIGHEST, preferred_element_type=F32) * inv_n
    yn = dev * lax.rsqrt(var + LNX_EPS) * lg_ref[...] + lb_ref[...]
    rw = (yn + bonus_ref[0] * v_ref[0]) * gate_ref[0]
    cat = jnp.concatenate([rw, pool_ref[0], four_ref[0]], axis=-1).astype(BF16)
    mix = jnp.dot(cat, wout_ref[...], preferred_element_type=F32)
    out_ref[0] = h_ref[0] + g1_ref[0] * mix


def _mix_out(y_f, y_b, v, bonus, gate, pool_out, four_out, h, lnx_g, lnx_b, w_out_bf16, g1, blk_off):
    B, T, D = h.shape
    W = RWKV_WIDTH
    tok = lambda w: pl.BlockSpec((1, TBLK, w), lambda b, j: (b, j, 0))
    rw_tok = pl.BlockSpec((1, TBLK, W), lambda b, j: (b, j + blk_off, 0))
    full = lambda shape: pl.BlockSpec(shape, lambda b, j: (0,) * len(shape))
    bd = jnp.asarray(np.kron(np.eye(RWKV_HEADS, dtype=np.float32), np.ones((RWKV_HEAD, RWKV_HEAD), np.float32)))
    return pl.pallas_call(
        _mix_out_kernel,
        grid=(B, T // TBLK),
        in_specs=[rw_tok] * 5 + [tok(POOL_WIDTH), tok(FOURIER_WIDTH), tok(D), full((1, W)), full((1, W)),
                                 full((W, W)), full((D, D)), pl.BlockSpec((1, 1, D), lambda b, j: (b, 0, 0))],
        out_specs=tok(D),
        out_shape=jax.ShapeDtypeStruct((B, T, D), F32),
        compiler_params=pltpu.CompilerParams(dimension_semantics=("arbitrary", "arbitrary"),
                                             vmem_limit_bytes=48 * 1024 * 1024),
        name="mix_out",
    )(y_f, y_b, v, bonus, gate, pool_out, four_out, h, lnx_g.reshape(1, W), lnx_b.reshape(1, W), bd, w_out_bf16, g1)


POOL_TB = 512
assert POOL_TB // GRID_W >= max(POOL_WINDOWS) // 2


def _box_sums(x_win, m_ref):
    parts = [p.astype(BF16) for p in _split3_exact(x_win)]
    cols = []
    for gi in range(POOL_GROUPS):
        sl = slice(gi * POOL_GC, (gi + 1) * POOL_GC)
        acc = None
        for p in parts:
            term = jnp.dot(m_ref[gi], p[:, sl], preferred_element_type=F32)
            acc = term if acc is None else acc + term
        cols.append(acc)
    return jnp.concatenate(cols, axis=-1)


def _clipped_extent(pos, win, n):
    return jnp.minimum(pos + win // 2, n) - jnp.maximum(pos - win // 2, 0)


def _pool_finish(box, u, count, pw_ref, ps_ref, out_ref):
    p = (box / count - u).astype(BF16)
    out_ref[0] = jnp.dot(p, pw_ref[...], preferred_element_type=F32) * ps_ref[...]


def _lane_group_select(shape, per_group):
    lane = lax.broadcasted_iota(jnp.int32, shape, 1) // POOL_GC
    out = per_group[POOL_GROUPS - 1]
    for gi in range(POOL_GROUPS - 2, -1, -1):
        out = jnp.where(lane == gi, per_group[gi], out)
    return out


def _pool2d_kernel(rows, prev_ref, cur_ref, next_ref, m_ref, pw_ref, ps_ref, out_ref):
    j = pl.program_id(1)
    nb = pl.num_programs(1)
    u = cur_ref[0]
    x_win = jnp.concatenate([jnp.where(j > 0, prev_ref[0], 0.0), u, jnp.where(j < nb - 1, next_ref[0], 0.0)], axis=0)
    box = _box_sums(x_win, m_ref)
    t = lax.broadcasted_iota(jnp.int32, u.shape, 0)
    row = j * (POOL_TB // GRID_W) + t // GRID_W
    col = t % GRID_W
    count = _lane_group_select(u.shape, [(_clipped_extent(row, w, rows) * _clipped_extent(col, w, GRID_W)).astype(F32)
                                         for w in POOL_WINDOWS])
    _pool_finish(box, u, count, pw_ref, ps_ref, out_ref)


def _pool1d_kernel(cur_ref, m_ref, pw_ref, ps_ref, out_ref):
    u = cur_ref[0]
    box = _box_sums(u, m_ref)
    t = lax.broadcasted_iota(jnp.int32, u.shape, 0)
    count = _lane_group_select(u.shape, [_clipped_extent(t, w, u.shape[0]).astype(F32) for w in POOL_WINDOWS])
    _pool_finish(box, u, count, pw_ref, ps_ref, out_ref)


def _pool_readout_weight(pool_w):
    return jax.scipy.linalg.block_diag(*[pool_w[g] for g in range(POOL_GROUPS)]).astype(BF16)


def _pool2d(u, rows, pool_w, pool_scale):
    B, T, C = u.shape
    nb = T // POOL_TB
    rb = POOL_TB // GRID_W
    t_out, t_in = np.arange(POOL_TB), np.arange(3 * POOL_TB)
    r_o, c_o = t_out // GRID_W, t_out % GRID_W
    r_i, c_i = t_in // GRID_W - rb, t_in % GRID_W
    m = np.stack([((r_i[None, :] >= r_o[:, None] - w // 2) & (r_i[None, :] < r_o[:, None] + w // 2)
                   & (c_i[None, :] >= c_o[:, None] - w // 2) & (c_i[None, :] < c_o[:, None] + w // 2))
                  for w in POOL_WINDOWS]).astype(np.float32)
    blk = lambda f: pl.BlockSpec((1, POOL_TB, C), f)
    full = lambda shape: pl.BlockSpec(shape, lambda b, j: (0,) * len(shape))
    return pl.pallas_call(
        functools.partial(_pool2d_kernel, rows),
        grid=(B, nb),
        in_specs=[blk(lambda b, j: (b, jnp.maximum(j - 1, 0), 0)), blk(lambda b, j: (b, j, 0)),
                  blk(lambda b, j: (b, jnp.minimum(j + 1, nb - 1), 0)),
                  full((POOL_GROUPS, POOL_TB, 3 * POOL_TB)), full((C, C)), full((1, C))],
        out_specs=blk(lambda b, j: (b, j, 0)),
        out_shape=jax.ShapeDtypeStruct((B, T, C), F32),
        compiler_params=pltpu.CompilerParams(dimension_semantics=("arbitrary", "arbitrary"),
                                             vmem_limit_bytes=48 * 1024 * 1024),
        name="pool2d",
    )(u, u, u, jnp.asarray(m, BF16), _pool_readout_weight(pool_w), pool_scale.reshape(1, C))


def _pool1d(u, pool_w, pool_scale):
    B, T, C = u.shape
    t = np.arange(T)
    m = np.stack([((t[None, :] >= t[:, None] - w // 2) & (t[None, :] < t[:, None] + w // 2))
                  for w in POOL_WINDOWS]).astype(np.float32)
    full = lambda shape: pl.BlockSpec(shape, lambda b: (0,) * len(shape))
    return pl.pallas_call(
        _pool1d_kernel,
        grid=(B,),
        in_specs=[pl.BlockSpec((1, T, C), lambda b: (b, 0, 0)), full((POOL_GROUPS, T, T)), full((C, C)), full((1, C))],
        out_specs=pl.BlockSpec((1, T, C), lambda b: (b, 0, 0)),
        out_shape=jax.ShapeDtypeStruct((B, T, C), F32),
        compiler_params=pltpu.CompilerParams(dimension_semantics=("arbitrary",)),
        name="pool1d",
    )(u, jnp.asarray(m, BF16), _pool_readout_weight(pool_w), pool_scale.reshape(1, C))


FOUR_N1 = 128
FOUR_LB = 2048


def _four_a_kernel(x_ref, fr_ref, fi_ref, yr_ref, yi_ref):
    x = x_ref[0]
    yr_ref[0] = jnp.dot(fr_ref[...], x, precision=HIGHEST, preferred_element_type=F32)
    yi_ref[0] = jnp.dot(fi_ref[...], x, precision=HIGHEST, preferred_element_type=F32)


def _four_b_kernel(yr_ref, yi_ref, fr_ref, fi_ref, twr_ref, twi_ref, cc_ref, ss_ref, w_ref, out_ref):
    c = pl.program_id(1)
    twr = twr_ref[pl.ds(c, 1), :]
    twi = twi_ref[pl.ds(c, 1), :]
    fr, fi = fr_ref[...], fi_ref[...]
    hr = fr * twr - fi * twi
    hi = fr * twi + fi * twr
    yr, yi = yr_ref[0], yi_ref[0]
    dot = lambda a, b: jnp.dot(a, b, precision=HIGHEST, preferred_element_type=F32)
    xr = dot(hr, yr) - dot(hi, yi)
    xi = dot(hr, yi) + dot(hi, yr)
    f = dot(xr, cc_ref[...]) + dot(xi, ss_ref[...])
    out_ref[0] = jnp.dot(f.astype(BF16), w_ref[...], preferred_element_type=F32)


def _four_ctx_kernel(x_ref, fr_ref, fi_ref, cc_ref, ss_ref, w_ref, out_ref):
    x = x_ref[0]
    dot = lambda a, b: jnp.dot(a, b, precision=HIGHEST, preferred_element_type=F32)
    f = dot(dot(fr_ref[...], x), cc_ref[...]) + dot(dot(fi_ref[...], x), ss_ref[...])
    out_ref[0] = jnp.dot(f.astype(BF16), w_ref[...], preferred_element_type=F32)


def _dft_parts(n, scale):
    ang = 2.0 * np.pi * np.outer(np.arange(n), np.arange(n)) / n
    return jnp.asarray(np.cos(ang) * scale, F32), jnp.asarray(-np.sin(ang) * scale, F32)


def _channel_dft():
    ang = 2.0 * np.pi * np.outer(np.arange(FOURIER_HC), np.arange(FOURIER_HC)) / FOURIER_HC
    eye = np.eye(FOURIER_HEADS)
    return jnp.asarray(np.kron(eye, np.cos(ang)), F32), jnp.asarray(np.kron(eye, np.sin(ang)), F32)


def _fourier_readout_weight(fourier_w):
    return jax.scipy.linalg.block_diag(*[fourier_w[h] for h in range(FOURIER_HEADS)]).astype(BF16)


def _fourier_lat(u, fourier_w):
    B, T, C = u.shape
    n1 = FOUR_N1
    assert T == n1 * n1
    scale = 1.0 / np.sqrt(float(T) * FOURIER_HC)
    fr_s, fi_s = _dft_parts(n1, scale)
    fr, fi = _dft_parts(n1, 1.0)
    ang = 2.0 * np.pi * np.outer(np.arange(n1), np.arange(n1)) / T
    twr, twi = jnp.asarray(np.cos(ang), F32), jnp.asarray(-np.sin(ang), F32)
    cc, ss = _channel_dft()
    sq = lambda n: pl.BlockSpec((n, n), lambda b, i: (0, 0))
    y_shape = jax.ShapeDtypeStruct((B, n1, n1 * C), F32)
    a_blk = pl.BlockSpec((1, n1, FOUR_LB), lambda b, i: (b, 0, i))
    yr, yi = pl.pallas_call(
        _four_a_kernel,
        grid=(B, n1 * C // FOUR_LB),
        in_specs=[a_blk, sq(n1), sq(n1)],
        out_specs=[a_blk, a_blk],
        out_shape=[y_shape, y_shape],
        compiler_params=pltpu.CompilerParams(dimension_semantics=("arbitrary", "arbitrary")),
        name="fourier_a",
    )(u.reshape(B, n1, n1 * C), fr_s, fi_s)
    y_blk = pl.BlockSpec((1, n1, C), lambda b, c: (b, c, 0))
    out = pl.pallas_call(
        _four_b_kernel,
        grid=(B, n1),
        in_specs=[y_blk, y_blk, sq(n1), sq(n1), sq(n1), sq(n1), sq(C), sq(C), sq(C)],
        out_specs=pl.BlockSpec((1, n1, C), lambda b, c: (b, 0, c)),
        out_shape=jax.ShapeDtypeStruct((B, n1, n1 * C), F32),
        compiler_params=pltpu.CompilerParams(dimension_semantics=("arbitrary", "arbitrary")),
        name="fourier_b",
    )(yr.reshape(B, T, C), yi.reshape(B, T, C), fr, fi, twr, twi, cc, ss, _fourier_readout_weight(fourier_w))
    return out.reshape(B, T, C)


def _fourier_ctx(u, fourier_w):
    B, T, C = u.shape
    fr, fi = _dft_parts(T, 1.0 / np.sqrt(float(T) * FOURIER_HC))
    cc, ss = _channel_dft()
    sq = lambda n: pl.BlockSpec((n, n), lambda b: (0, 0))
    blk = pl.BlockSpec((1, T, C), lambda b: (b, 0, 0))
    return pl.pallas_call(
        _four_ctx_kernel,
        grid=(B,),
        in_specs=[blk, sq(T), sq(T), sq(C), sq(C), sq(C)],
        out_specs=blk,
        out_shape=jax.ShapeDtypeStruct((B, T, C), F32),
        compiler_params=pltpu.CompilerParams(dimension_semantics=("arbitrary",)),
        name="fourier_ctx",
    )(u, fr, fi, cc, ss, _fourier_readout_weight(fourier_w))


def _token_mixing(h_lat, h_ctx, mod_lat, mod_ctx, rows, ctx_out, norm_g, w_in, shift_w, decay_w0, decay_up, iclr_a0,
                  iclr_up, gate_up, k_k, k_a, r_k, lnx_g, lnx_b, pool_w, pool_scale, fourier_w, w_out):
    ttot = CTX_LEN + h_lat.shape[1]
    w_in_b, w_out_b = w_in.astype(BF16), w_out.astype(BF16)
    prw, ppool_l, pfour_l = _mix_in(h_lat, norm_g, mod_lat[0], mod_lat[1], w_in_b, 1, ttot)
    prw, ppool_c, pfour_c = _mix_in(h_ctx, norm_g, mod_ctx[0], mod_ctx[1], w_in_b, 0, ttot, prw_all=prw)
    lt0, lt1, v, gate, bonus = _rwkv_prep(prw, shift_w, k_k, k_a, r_k, decay_w0, decay_up, iclr_a0, iclr_up, gate_up)
    y_f, y_b = _delta_scan(lt0, lt1, v)
    pool_l = _pool2d(ppool_l, rows, pool_w, pool_scale)
    four_l = _fourier_lat(pfour_l, fourier_w)
    out_lat = _mix_out(y_f, y_b, v, bonus, gate, pool_l, four_l, h_lat, lnx_g, lnx_b, w_out_b, mod_lat[2], 1)
    if not ctx_out:
        return out_lat, h_ctx
    pool_c = _pool1d(ppool_c, pool_w, pool_scale)
    four_c = _fourier_ctx(pfour_c, fourier_w)
    out_ctx = _mix_out(y_f, y_b, v, bonus, gate, pool_c, four_c, h_ctx, lnx_g, lnx_b, w_out_b, mod_ctx[2], 0)
    return out_lat, out_ctx


MOE_TB = 256
MOE_ALIGN = BF16_ROWS
MOE_WIN = MOE_TB + MOE_ALIGN
MOE_EG = 4
FFN_ROWS = 512


def _ffn_pre_kernel(h_ref, g_ref, sh_ref, sc_ref, rw_ref, rwt_ref, hn_ref, aff_ref, afft_ref):
    x = h_ref[0]
    xn = x * lax.rsqrt(jnp.mean(x * x, axis=-1, keepdims=True) + RMS_EPS) * g_ref[...]
    hn = xn * (1.0 + sc_ref[0]) + sh_ref[0]
    hn_ref[0] = hn.astype(BF16)
    logits = jnp.dot(hn, rw_ref[...], precision=HIGHEST, preferred_element_type=F32)
    ex = jnp.exp(logits - jnp.max(logits, axis=-1, keepdims=True))
    aff_ref[0] = ex / jnp.sum(ex, axis=-1, keepdims=True)
    logits_t = lax.dot_general(rwt_ref[...], hn, (((1,), (1,)), ((), ())), precision=HIGHEST,
                               preferred_element_type=F32)
    ext = jnp.exp(logits_t - jnp.max(logits_t, axis=0, keepdims=True))
    afft_ref[0] = ext / jnp.sum(ext, axis=0, keepdims=True)


def _ffn_pre(h, g, shift, scale, router_w):
    B, T, D = h.shape
    E = N_EXPERTS
    tok = lambda w: pl.BlockSpec((1, MOE_TB, w), lambda b, j: (b, j, 0))
    vec = pl.BlockSpec((1, 1, D), lambda b, j: (b, 0, 0))
    return pl.pallas_call(
        _ffn_pre_kernel,
        grid=(B, T // MOE_TB),
        in_specs=[tok(D), pl.BlockSpec((1, D), lambda b, j: (0, 0)), vec, vec,
                  pl.BlockSpec((D, E), lambda b, j: (0, 0)), pl.BlockSpec((E, D), lambda b, j: (0, 0))],
        out_specs=[tok(D), tok(E), pl.BlockSpec((1, E, MOE_TB), lambda b, j: (b, 0, j))],
        out_shape=[jax.ShapeDtypeStruct((B, T, D), BF16), jax.ShapeDtypeStruct((B, T, E), F32),
                   jax.ShapeDtypeStruct((B, E, T), F32)],
        compiler_params=pltpu.CompilerParams(dimension_semantics=("arbitrary", "arbitrary")),
        name="ffn_pre",
    )(h, g.reshape(1, D), shift, scale, router_w, router_w.T)


def _topk_kernel(cap, afft_ref, tri_ref, ones_ref, pos_ref, start_ref):
    x = afft_ref[0]
    E, T = x.shape
    xi = pltpu.bitcast(x, jnp.int32)
    capf = jnp.float32(cap)

    def count(mask):
        return jnp.sum(jnp.where(mask, 1.0, 0.0), axis=1, keepdims=True)

    thr = jnp.zeros((E, 1), jnp.int32)
    for bit in range(30, -1, -1):
        cand = thr | jnp.int32(1 << bit)
        thr = jnp.where(count(xi >= cand) >= capf, cand, thr)
    need = capf - count(xi > thr)
    tri = tri_ref[...]
    ones = ones_ref[...]
    eq_carry = jnp.zeros((E, MOE_TB), F32)
    sel_carry = jnp.zeros((E, MOE_TB), F32)
    for c in range(T // MOE_TB):
        xc = xi[:, c * MOE_TB:(c + 1) * MOE_TB]
        eq = jnp.where(xc == thr, 1.0, 0.0)
        eqb = eq.astype(BF16)
        eq_rank = eq_carry + jnp.dot(eqb, tri, preferred_element_type=F32) - eq
        eq_carry = eq_carry + jnp.dot(eqb, ones, preferred_element_type=F32)
        sel = jnp.where(jnp.logical_or(xc > thr, jnp.logical_and(xc == thr, eq_rank < need)), 1.0, 0.0)
        selb = sel.astype(BF16)
        slot = sel_carry + jnp.dot(selb, tri, preferred_element_type=F32) - sel
        pos_ref[0, :, c * MOE_TB:(c + 1) * MOE_TB] = jnp.where(sel > 0.0, slot, -1.0).astype(jnp.int32)
        start_ref[0, :, c:c + 1] = sel_carry[:, 0:1].astype(jnp.int32)
        sel_carry = sel_carry + jnp.dot(selb, ones, preferred_element_type=F32)


def _topk_select(afft, cap):
    B, E, T = afft.shape
    nb = T // MOE_TB
    tri = jnp.asarray(np.triu(np.ones((MOE_TB, MOE_TB), np.float32)), BF16)
    ones = jnp.ones((MOE_TB, MOE_TB), BF16)
    sq = pl.BlockSpec((MOE_TB, MOE_TB), lambda b: (0, 0))
    return pl.pallas_call(
        functools.partial(_topk_kernel, cap),
        grid=(B,),
        in_specs=[pl.BlockSpec((1, E, T), lambda b: (b, 0, 0)), sq, sq],
        out_specs=[pl.BlockSpec((1, E, T), lambda b: (b, 0, 0)), pl.BlockSpec((1, E, nb), lambda b: (b, 0, 0))],
        out_shape=[jax.ShapeDtypeStruct((B, E, T), jnp.int32), jax.ShapeDtypeStruct((B, E, nb), jnp.int32)],
        compiler_params=pltpu.CompilerParams(dimension_semantics=("arbitrary",),
                                             vmem_limit_bytes=48 * 1024 * 1024),
        name="topk_select",
    )(afft, tri, ones)


def _slot_onehot(pos_row, start):
    base = pl.multiple_of((start // MOE_ALIGN) * MOE_ALIGN, MOE_ALIGN)
    s_iota = lax.broadcasted_iota(jnp.int32, (MOE_WIN, MOE_TB), 0)
    onehot = jnp.where(pos_row - base == s_iota, 1.0, 0.0).astype(BF16)
    return onehot, base


def _expert_kernel(cap, start_ref, hn_ref, pos_ref, wg_ref, wu_ref, wd_ref, ye_ref, xe_ref):
    b, e, j = pl.program_id(0), pl.program_id(1), pl.program_id(2)
    nb = pl.num_programs(2)

    @pl.when(j == 0)
    def _():
        xe_ref[...] = jnp.zeros_like(xe_ref)

    onehot, base = _slot_onehot(pos_ref[0, 0], start_ref[(b * N_EXPERTS + e) * nb + j])
    xe_ref[pl.ds(base, MOE_WIN), :] += jnp.dot(onehot, hn_ref[0], preferred_element_type=F32)

    @pl.when(j == nb - 1)
    def _():
        rows = min(FFN_ROWS, cap)
        for r0 in range(0, cap, rows):
            xb = xe_ref[r0:r0 + rows, :].astype(BF16)
            gt = jnp.dot(xb, wg_ref[0], preferred_element_type=F32)
            up = jnp.dot(xb, wu_ref[0], preferred_element_type=F32)
            hid = (gt * _sigmoid(gt) * up).astype(BF16)
            ye_ref[0, 0, r0:r0 + rows, :] = jnp.dot(hid, wd_ref[0], preferred_element_type=F32).astype(BF16)
        ye_ref[0, 0, cap:, :] = jnp.zeros((MOE_WIN, ye_ref.shape[-1]), BF16)


def _expert_ffn(hn, pos4, start_flat, wg, wu, wd, cap):
    B, T, D = hn.shape
    E = N_EXPERTS
    capp = cap + MOE_WIN
    wspec = pl.BlockSpec((1, D, D), lambda b, e, j, s: (e, 0, 0))
    return pl.pallas_call(
        functools.partial(_expert_kernel, cap),
        grid_spec=pltpu.PrefetchScalarGridSpec(
            num_scalar_prefetch=1,
            grid=(B, E, T // MOE_TB),
            in_specs=[pl.BlockSpec((1, MOE_TB, D), lambda b, e, j, s: (b, j, 0)),
                      pl.BlockSpec((1, 1, 1, MOE_TB), lambda b, e, j, s: (b, e, 0, j)),
                      wspec, wspec, wspec],
            out_specs=pl.BlockSpec((1, 1, capp, D), lambda b, e, j, s: (b, e, 0, 0)),
            scratch_shapes=[pltpu.VMEM((capp, D), F32)],
        ),
        out_shape=jax.ShapeDtypeStruct((B, E, capp, D), BF16),
        compiler_params=pltpu.CompilerParams(dimension_semantics=("arbitrary", "arbitrary", "arbitrary"),
                                             vmem_limit_bytes=56 * 1024 * 1024),
        name="expert_ffn",
    )(start_flat, hn, pos4, wg, wu, wd)


def _combine_kernel(eg, start_ref, ye_ref, pos_ref, aff_ref, g2_ref, h_ref, out_ref):
    b, g, j = pl.program_id(0), pl.program_id(1), pl.program_id(2)
    nb = pl.num_programs(2)
    aff = aff_ref[0]
    lane = lax.broadcasted_iota(jnp.int32, aff.shape, 1)
    acc = jnp.zeros(h_ref.shape[1:], F32)
    for el in range(eg):
        e = g * eg + el
        onehot, base = _slot_onehot(pos_ref[0, el], start_ref[(b * N_EXPERTS + e) * nb + j])
        got = lax.dot_general(onehot, ye_ref[0, el, pl.ds(base, MOE_WIN), :], (((0,), (0,)), ((), ())),
                              preferred_element_type=F32)
        gate = jnp.sum(jnp.where(lane == e, aff, 0.0), axis=-1, keepdims=True)
        acc = acc + got * gate
    out_ref[0] = h_ref[0] + g2_ref[0] * acc


def _moe_combine(h, ye, pos4, aff, g2, start_flat):
    B, T, D = h.shape
    E = N_EXPERTS
    capp = ye.shape[2]
    nb = T // MOE_TB
    eg = MOE_EG if nb > 1 else E
    tok = lambda w: pl.BlockSpec((1, MOE_TB, w), lambda b, g, j, s: (b, j, 0))
    return pl.pallas_call(
        functools.partial(_combine_kernel, eg),
        grid_spec=pltpu.PrefetchScalarGridSpec(
            num_scalar_prefetch=1,
            grid=(B, E // eg, nb),
            in_specs=[pl.BlockSpec((1, eg, capp, D), lambda b, g, j, s: (b, g, 0, 0),
                                   pipeline_mode=pl.Buffered(1)),
                      pl.BlockSpec((1, eg, 1, MOE_TB), lambda b, g, j, s: (b, g, 0, j)),
                      tok(E), pl.BlockSpec((1, 1, D), lambda b, g, j, s: (b, 0, 0)), tok(D)],
            out_specs=tok(D),
        ),
        out_shape=jax.ShapeDtypeStruct((B, T, D), F32),
        input_output_aliases={5: 0},
        compiler_params=pltpu.CompilerParams(dimension_semantics=("arbitrary", "arbitrary", "arbitrary"),
                                             vmem_limit_bytes=56 * 1024 * 1024),
        name="moe_combine",
    )(start_flat, ye, pos4, aff, g2, h)


def _moe_block(h, norm_g, shift, scale, g2, router_w, wg, wu, wd):
    B, T, D = h.shape
    cap = CAPACITY_FACTOR * T // N_EXPERTS
    hn, aff, afft = _ffn_pre(h, norm_g, shift, scale, router_w)
    pos, start = _topk_select(afft, cap)
    pos4 = pos.reshape(B, N_EXPERTS, 1, T)
    start_flat = start.reshape(-1)
    ye = _expert_ffn(hn, pos4, start_flat, wg, wu, wd, cap)
    return _moe_combine(h, ye, pos4, aff, g2, start_flat)


def kernel(x, c, ctx, c_ctx, ada_w, ada_b, norm_mix_g, norm_ffn_g, w_in, shift_w, decay_w0, decay_up, iclr_a0,
           iclr_up, gate_up, k_k, k_a, r_k, lnx_g, lnx_b, pool_w, pool_scale, fourier_w, w_out, router_w,
           exp_w_gate, exp_w_up, exp_w_down, final_norm_g):
    rows = x.shape[1] // GRID_W
    h_lat, h_ctx = x, ctx
    for l in range(DEPTH):
        ctx_out = l < DEPTH - 1
        B, D = h_lat.shape[0], h_lat.shape[2]
        sh1_l, sc1_l, g1_l, sh2_l, sc2_l, g2_l = [m[:, None, :] for m in _adaln(c, ada_w[l], ada_b[l])]
        sh1_c, sc1_c, g1_c, sh2_c, sc2_c, g2_c = [jnp.broadcast_to(m[None, None, :], (B, 1, D))
                                                  for m in _adaln(c_ctx, ada_w[l], ada_b[l])]
        h_lat, h_ctx = _token_mixing(
            h_lat, h_ctx, (sh1_l, sc1_l, g1_l), (sh1_c, sc1_c, g1_c), rows, ctx_out, norm_mix_g[l],
            w_in[l], shift_w[l], decay_w0[l], decay_up[l], iclr_a0[l], iclr_up[l], gate_up[l],
            k_k[l], k_a[l], r_k[l], lnx_g[l], lnx_b[l], pool_w[l], pool_scale[l], fourier_w[l], w_out[l])
        experts = (exp_w_gate[l].astype(BF16), exp_w_up[l].astype(BF16), exp_w_down[l].astype(BF16))
        h_lat = _moe_block(h_lat, norm_ffn_g[l], sh2_l, sc2_l, g2_l, router_w[l], *experts)
        if ctx_out:
            h_ctx = _moe_block(h_ctx, norm_ffn_g[l], sh2_c, sc2_c, g2_c, router_w[l], *experts)
    return _rmsnorm(h_lat, final_norm_g)
```

```python
import functools

import numpy as np
import jax
import jax.numpy as jnp
from jax import lax
from jax.experimental import pallas as pl
from jax.experimental.pallas import tpu as pltpu

F32 = jnp.float32
BF16 = jnp.bfloat16
HIGHEST = lax.Precision.HIGHEST

D_MODEL = 1024
DEPTH = 2
GRID_W = 64
CTX_LEN = 256
RWKV_HEAD = 64
RWKV_WIDTH = 512
RWKV_HEADS = 8
DECAY_LORA = 64
ICLR_LORA = 64
GATE_LORA = 128
RWKV_COLS = 3 * RWKV_WIDTH + DECAY_LORA + ICLR_LORA + GATE_LORA
POOL_WIDTH = 256
POOL_WINDOWS = (2, 4, 8, 16)
POOL_GROUPS = 4
POOL_GC = 64
FOURIER_WIDTH = 256
FOURIER_HEADS = 4
FOURIER_HC = 64
IN_COLS = RWKV_COLS + POOL_WIDTH + FOURIER_WIDTH
MIX_SPLITS = (RWKV_COLS, RWKV_COLS + POOL_WIDTH)
N_EXPERTS = 16
CAPACITY_FACTOR = 2
N_MOD = 6
RMS_EPS = 1e-6
LNX_EPS = 64e-5

SUBLANES = 8
LANES = 128
BF16_ROWS = 16

SCAN_TC = 32
SCAN_SUBS = 8
TBLK = SCAN_TC * SCAN_SUBS
SCAN_NSEL = SCAN_TC // 2
SCAN_PARTS = 4
SCAN_KSEL = 2 * SCAN_PARTS * SCAN_TC
SCAN_NOPS = 5
SCAN_LROWS = SCAN_NOPS * RWKV_HEAD
KGROUPS = RWKV_HEAD // SUBLANES
HEAD_PAIRS = RWKV_HEADS // 2
assert TBLK == CTX_LEN and SCAN_PARTS * SCAN_TC == LANES


def _mirror_block(i, nblk):
    return jnp.where(i == 0, 0, nblk - i)


def _split3_exact(x):
    mask = jnp.uint32(0xFFFF0000)
    hi = pltpu.bitcast(pltpu.bitcast(x, jnp.uint32) & mask, F32)
    r1 = x - hi
    mid = pltpu.bitcast(pltpu.bitcast(r1, jnp.uint32) & mask, F32)
    return hi, mid, r1 - mid


def _head_sum(x, bd):
    acc = None
    for part in _split3_exact(x):
        term = jnp.dot(part.astype(BF16), bd, preferred_element_type=F32)
        acc = term if acc is None else acc + term
    return acc


def _head_blockdiag():
    return jnp.asarray(np.kron(np.eye(RWKV_HEADS, dtype=np.float32), np.ones((RWKV_HEAD, RWKV_HEAD), np.float32)), BF16)


def _sigmoid(x):
    return 1.0 / (1.0 + jnp.exp(-x))


def _softplus(x):
    return jnp.maximum(x, 0.0) + jnp.log(1.0 + jnp.exp(-jnp.abs(x)))


def _prep_kernel(p_ref, prev_ref, next_ref, sw_ref, kk_ref, ka_ref, rk_ref, w0_ref, a0_ref, dup_ref, aup_ref,
                 gup_ref, bd_ref, lt0_ref, lt1_ref, v_ref, gate_ref, bonus_ref):
    i = pl.program_id(1)
    nblk = pl.num_programs(1)
    prev_ok = i >= 2
    next_ok = jnp.logical_and(i >= 1, i <= nblk - 2)
    row = lax.broadcasted_iota(jnp.int32, (TBLK, 1), 0)

    def section(c0, c1):
        x = p_ref[0, :, c0:c1]
        prev_row = jnp.where(prev_ok, prev_ref[0, SUBLANES - 1:SUBLANES, c0:c1], 0.0)
        next_row = jnp.where(next_ok, next_ref[0, 0:1, c0:c1], 0.0)
        x_prev = jnp.where(row == 0, prev_row, pltpu.roll(x, 1, 0))
        x_next = jnp.where(row == TBLK - 1, next_row, pltpu.roll(x, TBLK - 1, 0))
        return x_prev * sw_ref[0:1, c0:c1] + x * sw_ref[1:2, c0:c1] + x_next * sw_ref[2:3, c0:c1]

    W = RWKV_WIDTH
    r = section(0, W)
    k = section(W, 2 * W)
    v = section(2 * W, 3 * W)
    xwa = section(3 * W, 3 * W + LANES)
    xg = section(3 * W + LANES, 3 * W + 2 * LANES)
    bd = bd_ref[...]

    kk = k * kk_ref[...]
    kk = kk * lax.rsqrt(_head_sum(kk * kk, bd) + 1e-12)
    v_ref[0] = v
    gate_ref[0] = jnp.dot(_sigmoid(xg), gup_ref[...], precision=HIGHEST, preferred_element_type=F32)
    txwa = jnp.tanh(xwa)
    a_neg = -kk

    def store_parts(lt_ref, d, o, x):
        parts = _split3_exact(x)
        for s in range(SCAN_SUBS):
            slot = s if d == 0 else SCAN_SUBS - 1 - s
            for p in range(3):
                lt_ref[0, slot, o, p * SCAN_TC:(p + 1) * SCAN_TC, :] = (
                    parts[p][s * SCAN_TC:(s + 1) * SCAN_TC, :].astype(BF16))
            lt_ref[0, slot, o, 3 * SCAN_TC:, :] = jnp.zeros((SCAN_TC, W), BF16)

    bonus = None
    for d, lt_ref in ((0, lt0_ref), (1, lt1_ref)):
        lw = jnp.dot(txwa, dup_ref[d], precision=HIGHEST, preferred_element_type=F32)
        w_log = -_softplus(-(w0_ref[d:d + 1, :] + lw)) - 0.5
        decay = jnp.exp(-jnp.exp(w_log))
        a_lr = _sigmoid(a0_ref[d:d + 1, :] + jnp.dot(xwa, aup_ref[d], precision=HIGHEST, preferred_element_type=F32))
        k_dir = k * (1.0 + (a_lr - 1.0) * ka_ref[...])
        bterm = _head_sum(r * k_dir * rk_ref[...], bd)
        bonus = bterm if bonus is None else bonus + bterm
        store_parts(lt_ref, d, 0, decay)
        store_parts(lt_ref, d, 1, a_neg)
        store_parts(lt_ref, d, 2, kk * a_lr)
        store_parts(lt_ref, d, 3, k_dir)
        store_parts(lt_ref, d, 4, r)
    bonus_ref[0] = bonus


def _rwkv_prep(p_all, shift_w, k_k, k_a, r_k, decay_w0, decay_up, iclr_a0, iclr_up, gate_up):
    B, TT, _ = p_all.shape
    nblk = TT // TBLK
    W = RWKV_WIDTH
    zeros = jnp.zeros((2, DECAY_LORA, W), F32)
    dup_pad = jnp.concatenate([decay_up, zeros], axis=1)
    aup_pad = jnp.concatenate([zeros, iclr_up], axis=1)
    bd = _head_blockdiag()
    rows8 = TT // SUBLANES
    full = lambda shape: pl.BlockSpec(shape, lambda b, i: (0,) * len(shape))
    lt_shape = jax.ShapeDtypeStruct((B, nblk * SCAN_SUBS, SCAN_NOPS, LANES, W), BF16)
    tok_shape = jax.ShapeDtypeStruct((B, TT, W), F32)
    tok_spec = pl.BlockSpec((1, TBLK, W), lambda b, i: (b, i, 0))
    return pl.pallas_call(
        _prep_kernel,
        grid=(B, nblk),
        in_specs=[
            pl.BlockSpec((1, TBLK, RWKV_COLS), lambda b, i: (b, i, 0)),
            pl.BlockSpec((1, SUBLANES, RWKV_COLS), lambda b, i: (b, jnp.maximum(i * (TBLK // SUBLANES) - 1, 0), 0)),
            pl.BlockSpec((1, SUBLANES, RWKV_COLS),
                         lambda b, i: (b, jnp.minimum((i + 1) * (TBLK // SUBLANES), rows8 - 1), 0)),
            full((3, RWKV_COLS)), full((1, W)), full((1, W)), full((1, W)), full((2, W)), full((2, W)),
            full((2, LANES, W)), full((2, LANES, W)), full((GATE_LORA, W)), full((W, W)),
        ],
        out_specs=[
            pl.BlockSpec((1, SCAN_SUBS, SCAN_NOPS, LANES, W), lambda b, i: (b, i, 0, 0, 0)),
            pl.BlockSpec((1, SCAN_SUBS, SCAN_NOPS, LANES, W), lambda b, i: (b, _mirror_block(i, nblk), 0, 0, 0)),
            tok_spec, tok_spec, tok_spec,
        ],
        out_shape=[lt_shape, lt_shape, tok_shape, tok_shape, tok_shape],
        compiler_params=pltpu.CompilerParams(
            dimension_semantics=("arbitrary", "arbitrary"),
            vmem_limit_bytes=48 * 1024 * 1024,
        ),
        name="rwkv_prep",
    )(p_all, p_all, p_all, shift_w, k_k.reshape(1, W), k_a.reshape(1, W), r_k.reshape(1, W), decay_w0, iclr_a0,
      dup_pad, aup_pad, gate_up, bd)


def _sublane_allsum(x):
    x = x + pltpu.roll(x, 4, 0)
    x = x + pltpu.roll(x, 2, 0)
    return x + pltpu.roll(x, 1, 0)


def _tree_sum(xs):
    xs = list(xs)
    while len(xs) > 1:
        xs = [xs[i] + xs[i + 1] for i in range(0, len(xs), 2)]
    return xs[0]


def _scan_kernel(lt0_ref, lt1_ref, ltn0_ref, ltn1_ref, sel_ref, vf_ref, vb_ref, yf_ref, yb_ref,
                 e0_ref, e1_ref, l_ref, s_ref):
    i = pl.program_id(1)
    lt_refs = (lt0_ref, lt1_ref)
    ltn_refs = (ltn0_ref, ltn1_ref)
    v_refs = (vf_ref, vb_ref)
    y_refs = (yf_ref, yb_ref)

    def build_l(g, src_ref, idx):
        for o in range(SCAN_NOPS):
            xt = src_ref[0, idx, o].T
            l_ref[g, o * RWKV_HEAD:(o + 1) * RWKV_HEAD, 0:LANES] = xt[0:RWKV_HEAD]
            l_ref[g, o * RWKV_HEAD:(o + 1) * RWKV_HEAD, LANES:2 * LANES] = xt[RWKV_HEAD:2 * RWKV_HEAD]

    def expand(e_ref, n):
        for g in range(2):
            e_ref[g, n] = jnp.dot(l_ref[g], sel_ref[g, n], preferred_element_type=F32)

    @pl.when(i == 0)
    def _():
        s_ref[...] = jnp.zeros_like(s_ref)
        for g in range(2):
            build_l(g, lt_refs[g], 0)
        for n in range(SCAN_NSEL):
            expand(e0_ref, n)

    def scan_sub(sub, e_cur, e_nxt):
        @pl.when(sub < SCAN_SUBS - 1)
        def _():
            for g in range(2):
                build_l(g, lt_refs[g], sub + 1)

        @pl.when(sub == SCAN_SUBS - 1)
        def _():
            for g in range(2):
                build_l(g, ltn_refs[g], 0)

        state = [[s_ref[g, SUBLANES * j:SUBLANES * (j + 1), :] for j in range(KGROUPS)] for g in range(2)]
        for n in range(SCAN_NSEL):
            expand(e_nxt, n)
            for s in range(2):
                t = sub * SCAN_TC + 2 * n + s
                for g in range(2):
                    st = state[g]
                    trow = t if g == 0 else TBLK - 1 - t

                    def col(o, j, g=g, s=s, n=n):
                        r0 = o * RWKV_HEAD + SUBLANES * j
                        return e_cur[g, n, r0:r0 + SUBLANES, s * LANES:(s + 1) * LANES]

                    sa = _sublane_allsum(_tree_sum(st[j] * col(1, j) for j in range(KGROUPS)))
                    v8 = jnp.broadcast_to(v_refs[g][0, pl.ds(trow, 1), :], (SUBLANES, LANES))
                    nst = [st[j] * col(0, j) + col(2, j) * sa + col(3, j) * v8 for j in range(KGROUPS)]
                    y8 = _sublane_allsum(_tree_sum(nst[j] * col(4, j) for j in range(KGROUPS)))
                    y_refs[g][0, pl.ds(trow, 1), :] = y8[0:1, :]
                    state[g] = nst
        for g in range(2):
            for j in range(KGROUPS):
                s_ref[g, SUBLANES * j:SUBLANES * (j + 1), :] = state[g][j]

    def sub_pair(q, carry):
        scan_sub(2 * q, e0_ref, e1_ref)
        scan_sub(2 * q + 1, e1_ref, e0_ref)
        return carry

    lax.fori_loop(0, SCAN_SUBS // 2, sub_pair, 0)


def _scan_selector():
    sel = np.zeros((2, SCAN_NSEL, 2, SCAN_PARTS, SCAN_TC, 2, LANES), np.float32)
    for g in range(2):
        for n in range(SCAN_NSEL):
            for s in range(2):
                step = 2 * n + s if g == 0 else SCAN_TC - 1 - (2 * n + s)
                for j in range(2):
                    sel[g, n, j, :3, step, s, j * RWKV_HEAD:(j + 1) * RWKV_HEAD] = 1.0
    return jnp.asarray(sel.reshape(2, SCAN_NSEL, SCAN_KSEL, 2 * LANES), BF16)


def _delta_scan(lt0, lt1, v_all):
    B, TT, W = v_all.shape
    nblk = TT // TBLK
    nsub = nblk * SCAN_SUBS
    lt_spec = pl.BlockSpec((1, SCAN_SUBS, SCAN_NOPS, LANES, LANES), lambda c, i: (c // HEAD_PAIRS, i, 0, 0, c % HEAD_PAIRS))
    ltn_spec = pl.BlockSpec((1, 1, SCAN_NOPS, LANES, LANES),
                            lambda c, i: (c // HEAD_PAIRS, jnp.minimum((i + 1) * SCAN_SUBS, nsub - 1), 0, 0, c % HEAD_PAIRS))
    f_spec = pl.BlockSpec((1, TBLK, LANES), lambda c, i: (c // HEAD_PAIRS, i, c % HEAD_PAIRS))
    b_spec = pl.BlockSpec((1, TBLK, LANES), lambda c, i: (c // HEAD_PAIRS, _mirror_block(i, nblk), c % HEAD_PAIRS))
    y_shape = jax.ShapeDtypeStruct((B, TT, W), F32)
    return pl.pallas_call(
        _scan_kernel,
        grid=(B * HEAD_PAIRS, nblk),
        in_specs=[lt_spec, lt_spec, ltn_spec, ltn_spec,
                  pl.BlockSpec((2, SCAN_NSEL, SCAN_KSEL, 2 * LANES), lambda c, i: (0, 0, 0, 0)),
                  f_spec, b_spec],
        out_specs=[f_spec, b_spec],
        out_shape=[y_shape, y_shape],
        scratch_shapes=[
            pltpu.VMEM((2, SCAN_NSEL, SCAN_LROWS, 2 * LANES), F32),
            pltpu.VMEM((2, SCAN_NSEL, SCAN_LROWS, 2 * LANES), F32),
            pltpu.VMEM((2, SCAN_LROWS, SCAN_KSEL), BF16),
            pltpu.VMEM((2, RWKV_HEAD, LANES), F32),
        ],
        compiler_params=pltpu.CompilerParams(
            dimension_semantics=("arbitrary", "arbitrary"),
            vmem_limit_bytes=52 * 1024 * 1024,
        ),
        name="delta_scan",
    )(lt0, lt1, lt0, lt1, _scan_selector(), v_all, v_all)


def _heads(t):
    return t.reshape(t.shape[0], t.shape[1], RWKV_HEADS, RWKV_HEAD)


def _rmsnorm(x, g):
    y = x * lax.rsqrt(jnp.mean(x * x, axis=-1, keepdims=True) + RMS_EPS)
    return y * g


def _adaln(cond, ada_w, ada_b):
    return jnp.split(jax.nn.silu(cond) @ ada_w + ada_b, N_MOD, axis=-1)


def _modulate(x, shift, scale):
    return x * (1 + scale) + shift


def _mix_in_kernel(h_ref, g_ref, sh_ref, sc_ref, w_ref, *rest):
    prw_ref, ppool_ref, pfour_ref = rest[-3:]
    x = h_ref[0]
    xn = x * lax.rsqrt(jnp.mean(x * x, axis=-1, keepdims=True) + RMS_EPS) * g_ref[...]
    hn = (xn * (1.0 + sc_ref[0]) + sh_ref[0]).astype(BF16)
    p = jnp.dot(hn, w_ref[...], preferred_element_type=F32)
    prw_ref[0] = p[:, :RWKV_COLS]
    ppool_ref[0] = p[:, RWKV_COLS:RWKV_COLS + POOL_WIDTH]
    pfour_ref[0] = p[:, RWKV_COLS + POOL_WIDTH:]


def _mix_in(h, g, shift, scale, w_in_bf16, blk_off, ttot, prw_all=None):
    B, T, D = h.shape
    tok = lambda w: pl.BlockSpec((1, TBLK, w), lambda b, j: (b, j, 0))
    vec = pl.BlockSpec((1, 1, D), lambda b, j: (b, 0, 0))
    in_specs = [tok(D), pl.BlockSpec((1, D), lambda b, j: (0, 0)), vec, vec,
                pl.BlockSpec((D, IN_COLS), lambda b, j: (0, 0))]
    args = [h, g.reshape(1, D), shift, scale, w_in_bf16]
    aliases = {}
    if prw_all is not None:
        in_specs.append(pl.BlockSpec(memory_space=pl.ANY))
        args.append(prw_all)
        aliases = {5: 0}
    return pl.pallas_call(
        _mix_in_kernel,
        grid=(B, T // TBLK),
        in_specs=in_specs,
        out_specs=[pl.BlockSpec((1, TBLK, RWKV_COLS), lambda b, j: (b, j + blk_off, 0)),
                   tok(POOL_WIDTH), tok(FOURIER_WIDTH)],
        out_shape=[jax.ShapeDtypeStruct((B, ttot, RWKV_COLS), F32), jax.ShapeDtypeStruct((B, T, POOL_WIDTH), F32),
                   jax.ShapeDtypeStruct((B, T, FOURIER_WIDTH), F32)],
        input_output_aliases=aliases,
        compiler_params=pltpu.CompilerParams(dimension_semantics=("arbitrary", "arbitrary"),
                                             vmem_limit_bytes=48 * 1024 * 1024),
        name="mix_in",
    )(*args)


def _mix_out_kernel(yf_ref, yb_ref, v_ref, bonus_ref, gate_ref, pool_ref, four_ref, h_ref, lg_ref, lb_ref, bd_ref,
                    wout_ref, g1_ref, out_ref):
    y = yf_ref[0] + yb_ref[0]
    bd = bd_ref[...]
    inv_n = 1.0 / RWKV_HEAD
    mu = _head_sum(y, bd) * inv_n
    dev = y - mu
    var = _head_sum(dev * dev, bd) * inv_n
    yn = dev * lax.rsqrt(var + LNX_EPS) * lg_ref[...] + lb_ref[...]
    rw = (yn + bonus_ref[0] * v_ref[0]) * gate_ref[0]
    cat = jnp.concatenate([rw, pool_ref[0], four_ref[0]], axis=-1).astype(BF16)
    mix = jnp.dot(cat, wout_ref[...], preferred_element_type=F32)
    out_ref[0] = h_ref[0] + g1_ref[0] * mix


def _mix_out(y_f, y_b, v, bonus, gate, pool_out, four_out, h, lnx_g, lnx_b, w_out_bf16, g1, blk_off):
    B, T, D = h.shape
    W = RWKV_WIDTH
    tok = lambda w: pl.BlockSpec((1, TBLK, w), lambda b, j: (b, j, 0))
    rw_tok = pl.BlockSpec((1, TBLK, W), lambda b, j: (b, j + blk_off, 0))
    full = lambda shape: pl.BlockSpec(shape, lambda b, j: (0,) * len(shape))
    bd = _head_blockdiag()
    return pl.pallas_call(
        _mix_out_kernel,
        grid=(B, T // TBLK),
        in_specs=[rw_tok] * 5 + [tok(POOL_WIDTH), tok(FOURIER_WIDTH), tok(D), full((1, W)), full((1, W)),
                                 full((W, W)), full((D, D)), pl.BlockSpec((1, 1, D), lambda b, j: (b, 0, 0))],
        out_specs=tok(D),
        out_shape=jax.ShapeDtypeStruct((B, T, D), F32),
        compiler_params=pltpu.CompilerParams(dimension_semantics=("arbitrary", "arbitrary"),
                                             vmem_limit_bytes=48 * 1024 * 1024),
        name="mix_out",
    )(y_f, y_b, v, bonus, gate, pool_out, four_out, h, lnx_g.reshape(1, W), lnx_b.reshape(1, W), bd, w_out_bf16, g1)


POOL_TB = 512
assert POOL_TB // GRID_W >= max(POOL_WINDOWS) // 2


def _box_sums(x_win, m_ref):
    parts = [p.astype(BF16) for p in _split3_exact(x_win)]
    cols = []
    for gi in range(POOL_GROUPS):
        sl = slice(gi * POOL_GC, (gi + 1) * POOL_GC)
        acc = None
        for p in parts:
            term = jnp.dot(m_ref[gi], p[:, sl], preferred_element_type=F32)
            acc = term if acc is None else acc + term
        cols.append(acc)
    return jnp.concatenate(cols, axis=-1)


def _clipped_extent(pos, win, n):
    return jnp.minimum(pos + win // 2, n) - jnp.maximum(pos - win // 2, 0)


def _pool_finish(box, u, count, pw_ref, ps_ref, out_ref):
    p = (box / count - u).astype(BF16)
    out_ref[0] = jnp.dot(p, pw_ref[...], preferred_element_type=F32) * ps_ref[...]


def _lane_group_select(shape, per_group):
    lane = lax.broadcasted_iota(jnp.int32, shape, 1) // POOL_GC
    out = per_group[POOL_GROUPS - 1]
    for gi in range(POOL_GROUPS - 2, -1, -1):
        out = jnp.where(lane == gi, per_group[gi], out)
    return out


def _pool2d_kernel(rows, prev_ref, cur_ref, next_ref, m_ref, pw_ref, ps_ref, out_ref):
    j = pl.program_id(1)
    nb = pl.num_programs(1)
    u = cur_ref[0]
    x_win = jnp.concatenate([jnp.where(j > 0, prev_ref[0], 0.0), u, jnp.where(j < nb - 1, next_ref[0], 0.0)], axis=0)
    box = _box_sums(x_win, m_ref)
    t = lax.broadcasted_iota(jnp.int32, u.shape, 0)
    row = j * (POOL_TB // GRID_W) + t // GRID_W
    col = t % GRID_W
    count = _lane_group_select(u.shape, [(_clipped_extent(row, w, rows) * _clipped_extent(col, w, GRID_W)).astype(F32)
                                         for w in POOL_WINDOWS])
    _pool_finish(box, u, count, pw_ref, ps_ref, out_ref)


def _pool1d_kernel(cur_ref, m_ref, pw_ref, ps_ref, out_ref):
    u = cur_ref[0]
    box = _box_sums(u, m_ref)
    t = lax.broadcasted_iota(jnp.int32, u.shape, 0)
    count = _lane_group_select(u.shape, [_clipped_extent(t, w, u.shape[0]).astype(F32) for w in POOL_WINDOWS])
    _pool_finish(box, u, count, pw_ref, ps_ref, out_ref)


def _pool_readout_weight(pool_w):
    return jax.scipy.linalg.block_diag(*[pool_w[g] for g in range(POOL_GROUPS)]).astype(BF16)


def _pool2d(u, rows, pool_w, pool_scale):
    B, T, C = u.shape
    nb = T // POOL_TB
    rb = POOL_TB // GRID_W
    t_out, t_in = np.arange(POOL_TB), np.arange(3 * POOL_TB)
    r_o, c_o = t_out // GRID_W, t_out % GRID_W
    r_i, c_i = t_in // GRID_W - rb, t_in % GRID_W
    m = np.stack([((r_i[None, :] >= r_o[:, None] - w // 2) & (r_i[None, :] < r_o[:, None] + w // 2)
                   & (c_i[None, :] >= c_o[:, None] - w // 2) & (c_i[None, :] < c_o[:, None] + w // 2))
                  for w in POOL_WINDOWS]).astype(np.float32)
    blk = lambda f: pl.BlockSpec((1, POOL_TB, C), f)
    full = lambda shape: pl.BlockSpec(shape, lambda b, j: (0,) * len(shape))
    return pl.pallas_call(
        functools.partial(_pool2d_kernel, rows),
        grid=(B, nb),
        in_specs=[blk(lambda b, j: (b, jnp.maximum(j - 1, 0), 0)), blk(lambda b, j: (b, j, 0)),
                  blk(lambda b, j: (b, jnp.minimum(j + 1, nb - 1), 0)),
                  full((POOL_GROUPS, POOL_TB, 3 * POOL_TB)), full((C, C)), full((1, C))],
        out_specs=blk(lambda b, j: (b, j, 0)),
        out_shape=jax.ShapeDtypeStruct((B, T, C), F32),
        compiler_params=pltpu.CompilerParams(dimension_semantics=("arbitrary", "arbitrary"),
                                             vmem_limit_bytes=48 * 1024 * 1024),
        name="pool2d",
    )(u, u, u, jnp.asarray(m, BF16), _pool_readout_weight(pool_w), pool_scale.reshape(1, C))


def _pool1d(u, pool_w, pool_scale):
    B, T, C = u.shape
    t = np.arange(T)
    m = np.stack([((t[None, :] >= t[:, None] - w // 2) & (t[None, :] < t[:, None] + w // 2))
                  for w in POOL_WINDOWS]).astype(np.float32)
    full = lambda shape: pl.BlockSpec(shape, lambda b: (0,) * len(shape))
    return pl.pallas_call(
        _pool1d_kernel,
        grid=(B,),
        in_specs=[pl.BlockSpec((1, T, C), lambda b: (b, 0, 0)), full((POOL_GROUPS, T, T)), full((C, C)), full((1, C))],
        out_specs=pl.BlockSpec((1, T, C), lambda b: (b, 0, 0)),
        out_shape=jax.ShapeDtypeStruct((B, T, C), F32),
        compiler_params=pltpu.CompilerParams(dimension_semantics=("arbitrary",)),
        name="pool1d",
    )(u, jnp.asarray(m, BF16), _pool_readout_weight(pool_w), pool_scale.reshape(1, C))


FOUR_N1 = 128
FOUR_LB = 2048


def _four_a_kernel(x_ref, fr_ref, fi_ref, yr_ref, yi_ref):
    x = x_ref[0]
    yr_ref[0] = jnp.dot(fr_ref[...], x, precision=HIGHEST, preferred_element_type=F32)
    yi_ref[0] = jnp.dot(fi_ref[...], x, precision=HIGHEST, preferred_element_type=F32)


def _four_b_kernel(yr_ref, yi_ref, fr_ref, fi_ref, twr_ref, twi_ref, cc_ref, ss_ref, w_ref, out_ref):
    c = pl.program_id(1)
    twr = twr_ref[pl.ds(c, 1), :]
    twi = twi_ref[pl.ds(c, 1), :]
    fr, fi = fr_ref[...], fi_ref[...]
    hr = fr * twr - fi * twi
    hi = fr * twi + fi * twr
    yr, yi = yr_ref[0], yi_ref[0]
    dot = lambda a, b: jnp.dot(a, b, precision=HIGHEST, preferred_element_type=F32)
    xr = dot(hr, yr) - dot(hi, yi)
    xi = dot(hr, yi) + dot(hi, yr)
    f = dot(xr, cc_ref[...]) + dot(xi, ss_ref[...])
    out_ref[0] = jnp.dot(f.astype(BF16), w_ref[...], preferred_element_type=F32)


def _four_ctx_kernel(x_ref, fr_ref, fi_ref, cc_ref, ss_ref, w_ref, out_ref):
    x = x_ref[0]
    dot = lambda a, b: jnp.dot(a, b, precision=HIGHEST, preferred_element_type=F32)
    f = dot(dot(fr_ref[...], x), cc_ref[...]) + dot(dot(fi_ref[...], x), ss_ref[...])
    out_ref[0] = jnp.dot(f.astype(BF16), w_ref[...], preferred_element_type=F32)


def _dft_parts(n, scale):
    ang = 2.0 * np.pi * np.outer(np.arange(n), np.arange(n)) / n
    return jnp.asarray(np.cos(ang) * scale, F32), jnp.asarray(-np.sin(ang) * scale, F32)


def _channel_dft():
    ang = 2.0 * np.pi * np.outer(np.arange(FOURIER_HC), np.arange(FOURIER_HC)) / FOURIER_HC
    eye = np.eye(FOURIER_HEADS)
    return jnp.asarray(np.kron(eye, np.cos(ang)), F32), jnp.asarray(np.kron(eye, np.sin(ang)), F32)


def _fourier_readout_weight(fourier_w):
    return jax.scipy.linalg.block_diag(*[fourier_w[h] for h in range(FOURIER_HEADS)]).astype(BF16)


def _fourier_lat(u, fourier_w):
    B, T, C = u.shape
    n1 = FOUR_N1
    assert T == n1 * n1
    scale = 1.0 / np.sqrt(float(T) * FOURIER_HC)
    fr_s, fi_s = _dft_parts(n1, scale)
    fr, fi = _dft_parts(n1, 1.0)
    ang = 2.0 * np.pi * np.outer(np.arange(n1), np.arange(n1)) / T
    twr, twi = jnp.asarray(np.cos(ang), F32), jnp.asarray(-np.sin(ang), F32)
    cc, ss = _channel_dft()
    sq = lambda n: pl.BlockSpec((n, n), lambda b, i: (0, 0))
    y_shape = jax.ShapeDtypeStruct((B, n1, n1 * C), F32)
    a_blk = pl.BlockSpec((1, n1, FOUR_LB), lambda b, i: (b, 0, i))
    yr, yi = pl.pallas_call(
        _four_a_kernel,
        grid=(B, n1 * C // FOUR_LB),
        in_specs=[a_blk, sq(n1), sq(n1)],
        out_specs=[a_blk, a_blk],
        out_shape=[y_shape, y_shape],
        compiler_params=pltpu.CompilerParams(dimension_semantics=("arbitrary", "arbitrary")),
        name="fourier_a",
    )(u.reshape(B, n1, n1 * C), fr_s, fi_s)
    y_blk = pl.BlockSpec((1, n1, C), lambda b, c: (b, c, 0))
    out = pl.pallas_call(
        _four_b_kernel,
        grid=(B, n1),
        in_specs=[y_blk, y_blk, sq(n1), sq(n1), sq(n1), sq(n1), sq(C), sq(C), sq(C)],
        out_specs=pl.BlockSpec((1, n1, C), lambda b, c: (b, 0, c)),
        out_shape=jax.ShapeDtypeStruct((B, n1, n1 * C), F32),
        compiler_params=pltpu.CompilerParams(dimension_semantics=("arbitrary", "arbitrary")),
        name="fourier_b",
    )(yr.reshape(B, T, C), yi.reshape(B, T, C), fr, fi, twr, twi, cc, ss, _fourier_readout_weight(fourier_w))
    return out.reshape(B, T, C)


def _fourier_ctx(u, fourier_w):
    B, T, C = u.shape
    fr, fi = _dft_parts(T, 1.0 / np.sqrt(float(T) * FOURIER_HC))
    cc, ss = _channel_dft()
    sq = lambda n: pl.BlockSpec((n, n), lambda b: (0, 0))
    blk = pl.BlockSpec((1, T, C), lambda b: (b, 0, 0))
    return pl.pallas_call(
        _four_ctx_kernel,
        grid=(B,),
        in_specs=[blk, sq(T), sq(T), sq(C), sq(C), sq(C)],
        out_specs=blk,
        out_shape=jax.ShapeDtypeStruct((B, T, C), F32),
        compiler_params=pltpu.CompilerParams(dimension_semantics=("arbitrary",)),
        name="fourier_ctx",
    )(u, fr, fi, cc, ss, _fourier_readout_weight(fourier_w))


def _token_mixing(h_lat, h_ctx, mod_lat, mod_ctx, rows, ctx_out, norm_g, w_in, shift_w, decay_w0, decay_up, iclr_a0,
                  iclr_up, gate_up, k_k, k_a, r_k, lnx_g, lnx_b, pool_w, pool_scale, fourier_w, w_out):
    ttot = CTX_LEN + h_lat.shape[1]
    w_in_b, w_out_b = w_in.astype(BF16), w_out.astype(BF16)
    prw, ppool_l, pfour_l = _mix_in(h_lat, norm_g, mod_lat[0], mod_lat[1], w_in_b, 1, ttot)
    prw, ppool_c, pfour_c = _mix_in(h_ctx, norm_g, mod_ctx[0], mod_ctx[1], w_in_b, 0, ttot, prw_all=prw)
    lt0, lt1, v, gate, bonus = _rwkv_prep(prw, shift_w, k_k, k_a, r_k, decay_w0, decay_up, iclr_a0, iclr_up, gate_up)
    y_f, y_b = _delta_scan(lt0, lt1, v)
    pool_l = _pool2d(ppool_l, rows, pool_w, pool_scale)
    four_l = _fourier_lat(pfour_l, fourier_w)
    out_lat = _mix_out(y_f, y_b, v, bonus, gate, pool_l, four_l, h_lat, lnx_g, lnx_b, w_out_b, mod_lat[2], 1)
    if not ctx_out:
        return out_lat, h_ctx
    pool_c = _pool1d(ppool_c, pool_w, pool_scale)
    four_c = _fourier_ctx(pfour_c, fourier_w)
    out_ctx = _mix_out(y_f, y_b, v, bonus, gate, pool_c, four_c, h_ctx, lnx_g, lnx_b, w_out_b, mod_ctx[2], 0)
    return out_lat, out_ctx


MOE_TB = 256
MOE_ALIGN = BF16_ROWS
MOE_WIN = MOE_TB + MOE_ALIGN
MOE_EG = 4
MOE_GATHER_BLOCKS = 4
FFN_ROWS = 512


def _ffn_pre_kernel(h_ref, g_ref, sh_ref, sc_ref, rw_ref, rwt_ref, hn_ref, aff_ref, afft_ref):
    x = h_ref[0]
    xn = x * lax.rsqrt(jnp.mean(x * x, axis=-1, keepdims=True) + RMS_EPS) * g_ref[...]
    hn = xn * (1.0 + sc_ref[0]) + sh_ref[0]
    hn_ref[0] = hn.astype(BF16)
    logits = jnp.dot(hn, rw_ref[...], precision=HIGHEST, preferred_element_type=F32)
    ex = jnp.exp(logits - jnp.max(logits, axis=-1, keepdims=True))
    aff_ref[0] = ex / jnp.sum(ex, axis=-1, keepdims=True)
    logits_t = lax.dot_general(rwt_ref[...], hn, (((1,), (1,)), ((), ())), precision=HIGHEST,
                               preferred_element_type=F32)
    ext = jnp.exp(logits_t - jnp.max(logits_t, axis=0, keepdims=True))
    afft_ref[0] = ext / jnp.sum(ext, axis=0, keepdims=True)


def _ffn_pre(h, g, shift, scale, router_w):
    B, T, D = h.shape
    E = N_EXPERTS
    tok = lambda w: pl.BlockSpec((1, MOE_TB, w), lambda b, j: (b, j, 0))
    vec = pl.BlockSpec((1, 1, D), lambda b, j: (b, 0, 0))
    return pl.pallas_call(
        _ffn_pre_kernel,
        grid=(B, T // MOE_TB),
        in_specs=[tok(D), pl.BlockSpec((1, D), lambda b, j: (0, 0)), vec, vec,
                  pl.BlockSpec((D, E), lambda b, j: (0, 0)), pl.BlockSpec((E, D), lambda b, j: (0, 0))],
        out_specs=[tok(D), tok(E), pl.BlockSpec((1, E, MOE_TB), lambda b, j: (b, 0, j))],
        out_shape=[jax.ShapeDtypeStruct((B, T, D), BF16), jax.ShapeDtypeStruct((B, T, E), F32),
                   jax.ShapeDtypeStruct((B, E, T), F32)],
        compiler_params=pltpu.CompilerParams(dimension_semantics=("arbitrary", "arbitrary")),
        name="ffn_pre",
    )(h, g.reshape(1, D), shift, scale, router_w, router_w.T)


def _topk_kernel(cap, afft_ref, tri_ref, ones_ref, pos_ref, start_ref):
    x = afft_ref[0]
    E, T = x.shape
    xi = pltpu.bitcast(x, jnp.int32)
    capf = jnp.float32(cap)

    def count(mask):
        return jnp.sum(jnp.where(mask, 1.0, 0.0), axis=1, keepdims=True)

    thr = jnp.zeros((E, 1), jnp.int32)
    for bit in range(30, -1, -1):
        cand = thr | jnp.int32(1 << bit)
        thr = jnp.where(count(xi >= cand) >= capf, cand, thr)
    need = capf - count(xi > thr)
    tri = tri_ref[...]
    ones = ones_ref[...]
    eq_carry = jnp.zeros((E, MOE_TB), F32)
    sel_carry = jnp.zeros((E, MOE_TB), F32)
    for c in range(T // MOE_TB):
        xc = xi[:, c * MOE_TB:(c + 1) * MOE_TB]
        eq = jnp.where(xc == thr, 1.0, 0.0)
        eqb = eq.astype(BF16)
        eq_rank = eq_carry + jnp.dot(eqb, tri, preferred_element_type=F32) - eq
        eq_carry = eq_carry + jnp.dot(eqb, ones, preferred_element_type=F32)
        sel = jnp.where(jnp.logical_or(xc > thr, jnp.logical_and(xc == thr, eq_rank < need)), 1.0, 0.0)
        selb = sel.astype(BF16)
        slot = sel_carry + jnp.dot(selb, tri, preferred_element_type=F32) - sel
        pos_ref[0, :, c * MOE_TB:(c + 1) * MOE_TB] = jnp.where(sel > 0.0, slot, -1.0).astype(jnp.int32)
        start_ref[0, :, c:c + 1] = sel_carry[:, 0:1].astype(jnp.int32)
        sel_carry = sel_carry + jnp.dot(selb, ones, preferred_element_type=F32)


def _topk_select(afft, cap):
    B, E, T = afft.shape
    nb = T // MOE_TB
    tri = jnp.asarray(np.triu(np.ones((MOE_TB, MOE_TB), np.float32)), BF16)
    ones = jnp.ones((MOE_TB, MOE_TB), BF16)
    sq = pl.BlockSpec((MOE_TB, MOE_TB), lambda b: (0, 0))
    return pl.pallas_call(
        functools.partial(_topk_kernel, cap),
        grid=(B,),
        in_specs=[pl.BlockSpec((1, E, T), lambda b: (b, 0, 0)), sq, sq],
        out_specs=[pl.BlockSpec((1, E, T), lambda b: (b, 0, 0)), pl.BlockSpec((1, E, nb), lambda b: (b, 0, 0))],
        out_shape=[jax.ShapeDtypeStruct((B, E, T), jnp.int32), jax.ShapeDtypeStruct((B, E, nb), jnp.int32)],
        compiler_params=pltpu.CompilerParams(dimension_semantics=("arbitrary",),
                                             vmem_limit_bytes=48 * 1024 * 1024),
        name="topk_select",
    )(afft, tri, ones)


def _slot_onehot(pos_row, start):
    base = pl.multiple_of((start // MOE_ALIGN) * MOE_ALIGN, MOE_ALIGN)
    s_iota = lax.broadcasted_iota(jnp.int32, (MOE_WIN, MOE_TB), 0)
    onehot = jnp.where(pos_row - base == s_iota, 1.0, 0.0).astype(BF16)
    return onehot, base


def _expert_kernel(cap, gb, start_ref, hn_ref, pos_ref, wg_ref, wu_ref, wd_ref, ye_ref, xe_ref):
    b, e, j = pl.program_id(0), pl.program_id(1), pl.program_id(2)
    nb = pl.num_programs(2)

    @pl.when(j == 0)
    def _():
        xe_ref[...] = jnp.zeros_like(xe_ref)

    for q in range(gb):
        tok = slice(q * MOE_TB, (q + 1) * MOE_TB)
        onehot, base = _slot_onehot(pos_ref[0, 0, :, tok], start_ref[((b * N_EXPERTS + e) * nb + j) * gb + q])
        xe_ref[pl.ds(base, MOE_WIN), :] += jnp.dot(onehot, hn_ref[0, tok, :], preferred_element_type=F32)

    @pl.when(j == nb - 1)
    def _():
        rows = min(FFN_ROWS, cap)
        for r0 in range(0, cap, rows):
            xb = xe_ref[r0:r0 + rows, :].astype(BF16)
            gt = jnp.dot(xb, wg_ref[0], preferred_element_type=F32)
            up = jnp.dot(xb, wu_ref[0], preferred_element_type=F32)
            hid = (gt * _sigmoid(gt) * up).astype(BF16)
            ye_ref[0, 0, r0:r0 + rows, :] = jnp.dot(hid, wd_ref[0], preferred_element_type=F32).astype(BF16)
        ye_ref[0, 0, cap:, :] = jnp.zeros((MOE_WIN, ye_ref.shape[-1]), BF16)


def _expert_ffn(hn, pos4, start_flat, wg, wu, wd, cap):
    B, T, D = hn.shape
    E = N_EXPERTS
    capp = cap + MOE_WIN
    gb = min(MOE_GATHER_BLOCKS, T // MOE_TB)
    wspec = pl.BlockSpec((1, D, D), lambda b, e, j, s: (e, 0, 0))
    return pl.pallas_call(
        functools.partial(_expert_kernel, cap, gb),
        grid_spec=pltpu.PrefetchScalarGridSpec(
            num_scalar_prefetch=1,
            grid=(B, E, T // (gb * MOE_TB)),
            in_specs=[pl.BlockSpec((1, gb * MOE_TB, D), lambda b, e, j, s: (b, j, 0)),
                      pl.BlockSpec((1, 1, 1, gb * MOE_TB), lambda b, e, j, s: (b, e, 0, j)),
                      wspec, wspec, wspec],
            out_specs=pl.BlockSpec((1, 1, capp, D), lambda b, e, j, s: (b, e, 0, 0)),
            scratch_shapes=[pltpu.VMEM((capp, D), F32)],
        ),
        out_shape=jax.ShapeDtypeStruct((B, E, capp, D), BF16),
        compiler_params=pltpu.CompilerParams(dimension_semantics=("arbitrary", "arbitrary", "arbitrary"),
                                             vmem_limit_bytes=56 * 1024 * 1024),
        name="expert_ffn",
    )(start_flat, hn, pos4, wg, wu, wd)


def _combine_kernel(eg, start_ref, ye_ref, pos_ref, aff_ref, g2_ref, h_ref, out_ref):
    b, g, j = pl.program_id(0), pl.program_id(1), pl.program_id(2)
    nb = pl.num_programs(2)
    aff = aff_ref[0]
    lane = lax.broadcasted_iota(jnp.int32, aff.shape, 1)
    acc = jnp.zeros(h_ref.shape[1:], F32)
    for el in range(eg):
        e = g * eg + el
        onehot, base = _slot_onehot(pos_ref[0, el], start_ref[(b * N_EXPERTS + e) * nb + j])
        got = lax.dot_general(onehot, ye_ref[0, el, pl.ds(base, MOE_WIN), :], (((0,), (0,)), ((), ())),
                              preferred_element_type=F32)
        gate = jnp.sum(jnp.where(lane == e, aff, 0.0), axis=-1, keepdims=True)
        acc = acc + got * gate
    out_ref[0] = h_ref[0] + g2_ref[0] * acc


def _moe_combine(h, ye, pos4, aff, g2, start_flat):
    B, T, D = h.shape
    E = N_EXPERTS
    capp = ye.shape[2]
    nb = T // MOE_TB
    eg = MOE_EG if nb > 1 else E
    tok = lambda w: pl.BlockSpec((1, MOE_TB, w), lambda b, g, j, s: (b, j, 0))
    return pl.pallas_call(
        functools.partial(_combine_kernel, eg),
        grid_spec=pltpu.PrefetchScalarGridSpec(
            num_scalar_prefetch=1,
            grid=(B, E // eg, nb),
            in_specs=[pl.BlockSpec((1, eg, capp, D), lambda b, g, j, s: (b, g, 0, 0),
                                   pipeline_mode=pl.Buffered(1)),
                      pl.BlockSpec((1, eg, 1, MOE_TB), lambda b, g, j, s: (b, g, 0, j)),
                      tok(E), pl.BlockSpec((1, 1, D), lambda b, g, j, s: (b, 0, 0)), tok(D)],
            out_specs=tok(D),
        ),
        out_shape=jax.ShapeDtypeStruct((B, T, D), F32),
        input_output_aliases={5: 0},
        compiler_params=pltpu.CompilerParams(dimension_semantics=("arbitrary", "arbitrary", "arbitrary"),
                                             vmem_limit_bytes=56 * 1024 * 1024),
        name="moe_combine",
    )(start_flat, ye, pos4, aff, g2, h)


def _moe_block(h, norm_g, shift, scale, g2, router_w, wg, wu, wd):
    B, T, D = h.shape
    cap = CAPACITY_FACTOR * T // N_EXPERTS
    hn, aff, afft = _ffn_pre(h, norm_g, shift, scale, router_w)
    pos, start = _topk_select(afft, cap)
    pos4 = pos.reshape(B, N_EXPERTS, 1, T)
    start_flat = start.reshape(-1)
    ye = _expert_ffn(hn, pos4, start_flat, wg, wu, wd, cap)
    return _moe_combine(h, ye, pos4, aff, g2, start_flat)


def kernel(x, c, ctx, c_ctx, ada_w, ada_b, norm_mix_g, norm_ffn_g, w_in, shift_w, decay_w0, decay_up, iclr_a0,
           iclr_up, gate_up, k_k, k_a, r_k, lnx_g, lnx_b, pool_w, pool_scale, fourier_w, w_out, router_w,
           exp_w_gate, exp_w_up, exp_w_down, final_norm_g):
    rows = x.shape[1] // GRID_W
    h_lat, h_ctx = x, ctx
    for l in range(DEPTH):
        ctx_out = l < DEPTH - 1
        B, D = h_lat.shape[0], h_lat.shape[2]
        sh1_l, sc1_l, g1_l, sh2_l, sc2_l, g2_l = [m[:, None, :] for m in _adaln(c, ada_w[l], ada_b[l])]
        sh1_c, sc1_c, g1_c, sh2_c, sc2_c, g2_c = [jnp.broadcast_to(m[None, None, :], (B, 1, D))
                                                  for m in _adaln(c_ctx, ada_w[l], ada_b[l])]
        h_lat, h_ctx = _token_mixing(
            h_lat, h_ctx, (sh1_l, sc1_l, g1_l), (sh1_c, sc1_c, g1_c), rows, ctx_out, norm_mix_g[l],
            w_in[l], shift_w[l], decay_w0[l], decay_up[l], iclr_a0[l], iclr_up[l], gate_up[l],
            k_k[l], k_a[l], r_k[l], lnx_g[l], lnx_b[l], pool_w[l], pool_scale[l], fourier_w[l], w_out[l])
        experts = (exp_w_gate[l].astype(BF16), exp_w_up[l].astype(BF16), exp_w_down[l].astype(BF16))
        h_lat = _moe_block(h_lat, norm_ffn_g[l], sh2_l, sc2_l, g2_l, router_w[l], *experts)
        if ctx_out:
            h_ctx = _moe_block(h_ctx, norm_ffn_g[l], sh2_c, sc2_c, g2_c, router_w[l], *experts)
    return _rmsnorm(h_lat, final_norm_g)
```

```python
import functools

import numpy as np
import jax
import jax.numpy as jnp
from jax import lax
from jax.experimental import pallas as pl
from jax.experimental.pallas import tpu as pltpu

F32 = jnp.float32
BF16 = jnp.bfloat16
HIGHEST = lax.Precision.HIGHEST

D_MODEL = 1024
DEPTH = 2
GRID_W = 64
CTX_LEN = 256
RWKV_HEAD = 64
RWKV_WIDTH = 512
RWKV_HEADS = 8
DECAY_LORA = 64
ICLR_LORA = 64
GATE_LORA = 128
RWKV_COLS = 3 * RWKV_WIDTH + DECAY_LORA + ICLR_LORA + GATE_LORA
POOL_WIDTH = 256
POOL_WINDOWS = (2, 4, 8, 16)
POOL_GROUPS = 4
POOL_GC = 64
FOURIER_WIDTH = 256
FOURIER_HEADS = 4
FOURIER_HC = 64
IN_COLS = RWKV_COLS + POOL_WIDTH + FOURIER_WIDTH
MIX_SPLITS = (RWKV_COLS, RWKV_COLS + POOL_WIDTH)
N_EXPERTS = 16
CAPACITY_FACTOR = 2
N_MOD = 6
RMS_EPS = 1e-6
LNX_EPS = 64e-5

SUBLANES = 8
LANES = 128
BF16_ROWS = 16

SCAN_TC = 32
SCAN_SUBS = 8
TBLK = SCAN_TC * SCAN_SUBS
SCAN_NSEL = SCAN_TC // 2
SCAN_PARTS = 4
SCAN_KSEL = 2 * SCAN_PARTS * SCAN_TC
SCAN_NOPS = 5
SCAN_LROWS = SCAN_NOPS * RWKV_HEAD
KGROUPS = RWKV_HEAD // SUBLANES
HEAD_PAIRS = RWKV_HEADS // 2
assert TBLK == CTX_LEN and SCAN_PARTS * SCAN_TC == LANES


def _mirror_block(i, nblk):
    return jnp.where(i == 0, 0, nblk - i)


def _split3_exact(x):
    mask = jnp.uint32(0xFFFF0000)
    hi = pltpu.bitcast(pltpu.bitcast(x, jnp.uint32) & mask, F32)
    r1 = x - hi
    mid = pltpu.bitcast(pltpu.bitcast(r1, jnp.uint32) & mask, F32)
    return hi, mid, r1 - mid


def _head_sum(x, bd):
    acc = None
    for part in _split3_exact(x):
        term = jnp.dot(part.astype(BF16), bd, preferred_element_type=F32)
        acc = term if acc is None else acc + term
    return acc


def _head_blockdiag():
    return jnp.asarray(np.kron(np.eye(RWKV_HEADS, dtype=np.float32), np.ones((RWKV_HEAD, RWKV_HEAD), np.float32)), BF16)


def _sigmoid(x):
    return 1.0 / (1.0 + jnp.exp(-x))


def _softplus(x):
    return jnp.maximum(x, 0.0) + jnp.log(1.0 + jnp.exp(-jnp.abs(x)))


def _prep_kernel(p_ref, prev_ref, next_ref, sw_ref, kk_ref, ka_ref, rk_ref, w0_ref, a0_ref, dup_ref, aup_ref,
                 gup_ref, bd_ref, lt0_ref, lt1_ref, v_ref, gate_ref, bonus_ref):
    i = pl.program_id(1)
    nblk = pl.num_programs(1)
    prev_ok = i >= 2
    next_ok = jnp.logical_and(i >= 1, i <= nblk - 2)
    row = lax.broadcasted_iota(jnp.int32, (TBLK, 1), 0)

    def section(c0, c1):
        x = p_ref[0, :, c0:c1]
        prev_row = jnp.where(prev_ok, prev_ref[0, SUBLANES - 1:SUBLANES, c0:c1], 0.0)
        next_row = jnp.where(next_ok, next_ref[0, 0:1, c0:c1], 0.0)
        x_prev = jnp.where(row == 0, prev_row, pltpu.roll(x, 1, 0))
        x_next = jnp.where(row == TBLK - 1, next_row, pltpu.roll(x, TBLK - 1, 0))
        return x_prev * sw_ref[0:1, c0:c1] + x * sw_ref[1:2, c0:c1] + x_next * sw_ref[2:3, c0:c1]

    W = RWKV_WIDTH
    r = section(0, W)
    k = section(W, 2 * W)
    v = section(2 * W, 3 * W)
    xwa = section(3 * W, 3 * W + LANES)
    xg = section(3 * W + LANES, 3 * W + 2 * LANES)
    bd = bd_ref[...]

    kk = k * kk_ref[...]
    kk = kk * lax.rsqrt(_head_sum(kk * kk, bd) + 1e-12)
    v_ref[0] = v
    gate_ref[0] = jnp.dot(_sigmoid(xg), gup_ref[...], precision=HIGHEST, preferred_element_type=F32)
    txwa = jnp.tanh(xwa)
    a_neg = -kk

    def store_parts(lt_ref, d, o, x):
        parts = _split3_exact(x)
        for s in range(SCAN_SUBS):
            slot = s if d == 0 else SCAN_SUBS - 1 - s
            for p in range(3):
                lt_ref[0, slot, o, p * SCAN_TC:(p + 1) * SCAN_TC, :] = (
                    parts[p][s * SCAN_TC:(s + 1) * SCAN_TC, :].astype(BF16))
            lt_ref[0, slot, o, 3 * SCAN_TC:, :] = jnp.zeros((SCAN_TC, W), BF16)

    bonus = None
    for d, lt_ref in ((0, lt0_ref), (1, lt1_ref)):
        lw = jnp.dot(txwa, dup_ref[d], precision=HIGHEST, preferred_element_type=F32)
        w_log = -_softplus(-(w0_ref[d:d + 1, :] + lw)) - 0.5
        decay = jnp.exp(-jnp.exp(w_log))
        a_lr = _sigmoid(a0_ref[d:d + 1, :] + jnp.dot(xwa, aup_ref[d], precision=HIGHEST, preferred_element_type=F32))
        k_dir = k * (1.0 + (a_lr - 1.0) * ka_ref[...])
        bterm = _head_sum(r * k_dir * rk_ref[...], bd)
        bonus = bterm if bonus is None else bonus + bterm
        store_parts(lt_ref, d, 0, decay)
        store_parts(lt_ref, d, 1, a_neg)
        store_parts(lt_ref, d, 2, kk * a_lr)
        store_parts(lt_ref, d, 3, k_dir)
        store_parts(lt_ref, d, 4, r)
    bonus_ref[0] = bonus


def _rwkv_prep(p_all, shift_w, k_k, k_a, r_k, decay_w0, decay_up, iclr_a0, iclr_up, gate_up):
    B, TT, _ = p_all.shape
    nblk = TT // TBLK
    W = RWKV_WIDTH
    zeros = jnp.zeros((2, DECAY_LORA, W), F32)
    dup_pad = jnp.concatenate([decay_up, zeros], axis=1)
    aup_pad = jnp.concatenate([zeros, iclr_up], axis=1)
    bd = _head_blockdiag()
    rows8 = TT // SUBLANES
    full = lambda shape: pl.BlockSpec(shape, lambda b, i: (0,) * len(shape))
    lt_shape = jax.ShapeDtypeStruct((B, nblk * SCAN_SUBS, SCAN_NOPS, LANES, W), BF16)
    tok_shape = jax.ShapeDtypeStruct((B, TT, W), F32)
    tok_spec = pl.BlockSpec((1, TBLK, W), lambda b, i: (b, i, 0))
    return pl.pallas_call(
        _prep_kernel,
        grid=(B, nblk),
        in_specs=[
            pl.BlockSpec((1, TBLK, RWKV_COLS), lambda b, i: (b, i, 0)),
            pl.BlockSpec((1, SUBLANES, RWKV_COLS), lambda b, i: (b, jnp.maximum(i * (TBLK // SUBLANES) - 1, 0), 0)),
            pl.BlockSpec((1, SUBLANES, RWKV_COLS),
                         lambda b, i: (b, jnp.minimum((i + 1) * (TBLK // SUBLANES), rows8 - 1), 0)),
            full((3, RWKV_COLS)), full((1, W)), full((1, W)), full((1, W)), full((2, W)), full((2, W)),
            full((2, LANES, W)), full((2, LANES, W)), full((GATE_LORA, W)), full((W, W)),
        ],
        out_specs=[
            pl.BlockSpec((1, SCAN_SUBS, SCAN_NOPS, LANES, W), lambda b, i: (b, i, 0, 0, 0)),
            pl.BlockSpec((1, SCAN_SUBS, SCAN_NOPS, LANES, W), lambda b, i: (b, _mirror_block(i, nblk), 0, 0, 0)),
            tok_spec, tok_spec, tok_spec,
        ],
        out_shape=[lt_shape, lt_shape, tok_shape, tok_shape, tok_shape],
        compiler_params=pltpu.CompilerParams(
            dimension_semantics=("arbitrary", "arbitrary"),
            vmem_limit_bytes=48 * 1024 * 1024,
        ),
        name="rwkv_prep",
    )(p_all, p_all, p_all, shift_w, k_k.reshape(1, W), k_a.reshape(1, W), r_k.reshape(1, W), decay_w0, iclr_a0,
      dup_pad, aup_pad, gate_up, bd)


def _sublane_allsum(x):
    x = x + pltpu.roll(x, 4, 0)
    x = x + pltpu.roll(x, 2, 0)
    return x + pltpu.roll(x, 1, 0)


def _tree_sum(xs):
    xs = list(xs)
    while len(xs) > 1:
        xs = [xs[i] + xs[i + 1] for i in range(0, len(xs), 2)]
    return xs[0]


def _scan_kernel(lt0_ref, lt1_ref, ltn0_ref, ltn1_ref, sel_ref, vf_ref, vb_ref, yf_ref, yb_ref,
                 e0_ref, e1_ref, l_ref, s_ref):
    i = pl.program_id(1)
    lt_refs = (lt0_ref, lt1_ref)
    ltn_refs = (ltn0_ref, ltn1_ref)
    v_refs = (vf_ref, vb_ref)
    y_refs = (yf_ref, yb_ref)

    def build_l(g, src_ref, idx):
        for o in range(SCAN_NOPS):
            xt = src_ref[0, idx, o].T
            l_ref[g, o * RWKV_HEAD:(o + 1) * RWKV_HEAD, 0:LANES] = xt[0:RWKV_HEAD]
            l_ref[g, o * RWKV_HEAD:(o + 1) * RWKV_HEAD, LANES:2 * LANES] = xt[RWKV_HEAD:2 * RWKV_HEAD]

    def expand(e_ref, n):
        for g in range(2):
            e_ref[g, n] = jnp.dot(l_ref[g], sel_ref[g, n], preferred_element_type=F32)

    @pl.when(i == 0)
    def _():
        s_ref[...] = jnp.zeros_like(s_ref)
        for g in range(2):
            build_l(g, lt_refs[g], 0)
        for n in range(SCAN_NSEL):
            expand(e0_ref, n)

    def scan_sub(sub, e_cur, e_nxt):
        @pl.when(sub < SCAN_SUBS - 1)
        def _():
            for g in range(2):
                build_l(g, lt_refs[g], sub + 1)

        @pl.when(sub == SCAN_SUBS - 1)
        def _():
            for g in range(2):
                build_l(g, ltn_refs[g], 0)

        state = [[s_ref[g, SUBLANES * j:SUBLANES * (j + 1), :] for j in range(KGROUPS)] for g in range(2)]
        for n in range(SCAN_NSEL):
            expand(e_nxt, n)
            for s in range(2):
                t = sub * SCAN_TC + 2 * n + s
                for g in range(2):
                    st = state[g]
                    trow = t if g == 0 else TBLK - 1 - t

                    def col(o, j, g=g, s=s, n=n):
                        r0 = o * RWKV_HEAD + SUBLANES * j
                        return e_cur[g, n, r0:r0 + SUBLANES, s * LANES:(s + 1) * LANES]

                    sa = _sublane_allsum(_tree_sum(st[j] * col(1, j) for j in range(KGROUPS)))
                    v8 = jnp.broadcast_to(v_refs[g][0, pl.ds(trow, 1), :], (SUBLANES, LANES))
                    nst = [st[j] * col(0, j) + col(2, j) * sa + col(3, j) * v8 for j in range(KGROUPS)]
                    y8 = _sublane_allsum(_tree_sum(nst[j] * col(4, j) for j in range(KGROUPS)))
                    y_refs[g][0, pl.ds(trow, 1), :] = y8[0:1, :]
                    state[g] = nst
        for g in range(2):
            for j in range(KGROUPS):
                s_ref[g, SUBLANES * j:SUBLANES * (j + 1), :] = state[g][j]

    def sub_pair(q, carry):
        scan_sub(2 * q, e0_ref, e1_ref)
        scan_sub(2 * q + 1, e1_ref, e0_ref)
        return carry

    lax.fori_loop(0, SCAN_SUBS // 2, sub_pair, 0)


def _scan_selector():
    sel = np.zeros((2, SCAN_NSEL, 2, SCAN_PARTS, SCAN_TC, 2, LANES), np.float32)
    for g in range(2):
        for n in range(SCAN_NSEL):
            for s in range(2):
                step = 2 * n + s if g == 0 else SCAN_TC - 1 - (2 * n + s)
                for j in range(2):
                    sel[g, n, j, :3, step, s, j * RWKV_HEAD:(j + 1) * RWKV_HEAD] = 1.0
    return jnp.asarray(sel.reshape(2, SCAN_NSEL, SCAN_KSEL, 2 * LANES), BF16)


def _delta_scan(lt0, lt1, v_all):
    B, TT, W = v_all.shape
    nblk = TT // TBLK
    nsub = nblk * SCAN_SUBS
    lt_spec = pl.BlockSpec((1, SCAN_SUBS, SCAN_NOPS, LANES, LANES), lambda c, i: (c // HEAD_PAIRS, i, 0, 0, c % HEAD_PAIRS))
    ltn_spec = pl.BlockSpec((1, 1, SCAN_NOPS, LANES, LANES),
                            lambda c, i: (c // HEAD_PAIRS, jnp.minimum((i + 1) * SCAN_SUBS, nsub - 1), 0, 0, c % HEAD_PAIRS))
    f_spec = pl.BlockSpec((1, TBLK, LANES), lambda c, i: (c // HEAD_PAIRS, i, c % HEAD_PAIRS))
    b_spec = pl.BlockSpec((1, TBLK, LANES), lambda c, i: (c // HEAD_PAIRS, _mirror_block(i, nblk), c % HEAD_PAIRS))
    y_shape = jax.ShapeDtypeStruct((B, TT, W), F32)
    return pl.pallas_call(
        _scan_kernel,
        grid=(B * HEAD_PAIRS, nblk),
        in_specs=[lt_spec, lt_spec, ltn_spec, ltn_spec,
                  pl.BlockSpec((2, SCAN_NSEL, SCAN_KSEL, 2 * LANES), lambda c, i: (0, 0, 0, 0)),
                  f_spec, b_spec],
        out_specs=[f_spec, b_spec],
        out_shape=[y_shape, y_shape],
        scratch_shapes=[
            pltpu.VMEM((2, SCAN_NSEL, SCAN_LROWS, 2 * LANES), F32),
            pltpu.VMEM((2, SCAN_NSEL, SCAN_LROWS, 2 * LANES), F32),
            pltpu.VMEM((2, SCAN_LROWS, SCAN_KSEL), BF16),
            pltpu.VMEM((2, RWKV_HEAD, LANES), F32),
        ],
        compiler_params=pltpu.CompilerParams(
            dimension_semantics=("arbitrary", "arbitrary"),
            vmem_limit_bytes=52 * 1024 * 1024,
        ),
        name="delta_scan",
    )(lt0, lt1, lt0, lt1, _scan_selector(), v_all, v_all)


def _heads(t):
    return t.reshape(t.shape[0], t.shape[1], RWKV_HEADS, RWKV_HEAD)


def _rmsnorm(x, g):
    y = x * lax.rsqrt(jnp.mean(x * x, axis=-1, keepdims=True) + RMS_EPS)
    return y * g


def _adaln(cond, ada_w, ada_b):
    return jnp.split(jax.nn.silu(cond) @ ada_w + ada_b, N_MOD, axis=-1)


def _modulate(x, shift, scale):
    return x * (1 + scale) + shift


def _mix_in_kernel(h_ref, g_ref, sh_ref, sc_ref, w_ref, *rest):
    prw_ref, ppool_ref, pfour_ref = rest[-3:]
    x = h_ref[0]
    xn = x * lax.rsqrt(jnp.mean(x * x, axis=-1, keepdims=True) + RMS_EPS) * g_ref[...]
    hn = (xn * (1.0 + sc_ref[0]) + sh_ref[0]).astype(BF16)
    p = jnp.dot(hn, w_ref[...], preferred_element_type=F32)
    prw_ref[0] = p[:, :RWKV_COLS]
    ppool_ref[0] = p[:, RWKV_COLS:RWKV_COLS + POOL_WIDTH]
    pfour_ref[0] = p[:, RWKV_COLS + POOL_WIDTH:]


def _mix_in(h, g, shift, scale, w_in_bf16, blk_off, ttot, prw_all=None):
    B, T, D = h.shape
    tok = lambda w: pl.BlockSpec((1, TBLK, w), lambda b, j: (b, j, 0))
    vec = pl.BlockSpec((1, 1, D), lambda b, j: (b, 0, 0))
    in_specs = [tok(D), pl.BlockSpec((1, D), lambda b, j: (0, 0)), vec, vec,
                pl.BlockSpec((D, IN_COLS), lambda b, j: (0, 0))]
    args = [h, g.reshape(1, D), shift, scale, w_in_bf16]
    aliases = {}
    if prw_all is not None:
        in_specs.append(pl.BlockSpec(memory_space=pl.ANY))
        args.append(prw_all)
        aliases = {5: 0}
    return pl.pallas_call(
        _mix_in_kernel,
        grid=(B, T // TBLK),
        in_specs=in_specs,
        out_specs=[pl.BlockSpec((1, TBLK, RWKV_COLS), lambda b, j: (b, j + blk_off, 0)),
                   tok(POOL_WIDTH), tok(FOURIER_WIDTH)],
        out_shape=[jax.ShapeDtypeStruct((B, ttot, RWKV_COLS), F32), jax.ShapeDtypeStruct((B, T, POOL_WIDTH), F32),
                   jax.ShapeDtypeStruct((B, T, FOURIER_WIDTH), F32)],
        input_output_aliases=aliases,
        compiler_params=pltpu.CompilerParams(dimension_semantics=("arbitrary", "arbitrary"),
                                             vmem_limit_bytes=48 * 1024 * 1024),
        name="mix_in",
    )(*args)


def _mix_out_kernel(yf_ref, yb_ref, v_ref, bonus_ref, gate_ref, pool_ref, four_ref, h_ref, lg_ref, lb_ref, bd_ref,
                    wout_ref, g1_ref, out_ref):
    y = yf_ref[0] + yb_ref[0]
    bd = bd_ref[...]
    inv_n = 1.0 / RWKV_HEAD
    mu = _head_sum(y, bd) * inv_n
    dev = y - mu
    var = _head_sum(dev * dev, bd) * inv_n
    yn = dev * lax.rsqrt(var + LNX_EPS) * lg_ref[...] + lb_ref[...]
    rw = (yn + bonus_ref[0] * v_ref[0]) * gate_ref[0]
    cat = jnp.concatenate([rw, pool_ref[0], four_ref[0]], axis=-1).astype(BF16)
    mix = jnp.dot(cat, wout_ref[...], preferred_element_type=F32)
    out_ref[0] = h_ref[0] + g1_ref[0] * mix


def _mix_out(y_f, y_b, v, bonus, gate, pool_out, four_out, h, lnx_g, lnx_b, w_out_bf16, g1, blk_off):
    B, T, D = h.shape
    W = RWKV_WIDTH
    tok = lambda w: pl.BlockSpec((1, TBLK, w), lambda b, j: (b, j, 0))
    rw_tok = pl.BlockSpec((1, TBLK, W), lambda b, j: (b, j + blk_off, 0))
    full = lambda shape: pl.BlockSpec(shape, lambda b, j: (0,) * len(shape))
    bd = _head_blockdiag()
    return pl.pallas_call(
        _mix_out_kernel,
        grid=(B, T // TBLK),
        in_specs=[rw_tok] * 5 + [tok(POOL_WIDTH), tok(FOURIER_WIDTH), tok(D), full((1, W)), full((1, W)),
                                 full((W, W)), full((D, D)), pl.BlockSpec((1, 1, D), lambda b, j: (b, 0, 0))],
        out_specs=tok(D),
        out_shape=jax.ShapeDtypeStruct((B, T, D), F32),
        compiler_params=pltpu.CompilerParams(dimension_semantics=("arbitrary", "arbitrary"),
                                             vmem_limit_bytes=48 * 1024 * 1024),
        name="mix_out",
    )(y_f, y_b, v, bonus, gate, pool_out, four_out, h, lnx_g.reshape(1, W), lnx_b.reshape(1, W), bd, w_out_bf16, g1)


POOL_TB = 512
assert POOL_TB // GRID_W >= max(POOL_WINDOWS) // 2


def _box_sums(x_win, m_ref):
    parts = [p.astype(BF16) for p in _split3_exact(x_win)]
    cols = []
    for gi in range(POOL_GROUPS):
        sl = slice(gi * POOL_GC, (gi + 1) * POOL_GC)
        acc = None
        for p in parts:
            term = jnp.dot(m_ref[gi], p[:, sl], preferred_element_type=F32)
            acc = term if acc is None else acc + term
        cols.append(acc)
    return jnp.concatenate(cols, axis=-1)


def _clipped_extent(pos, win, n):
    return jnp.minimum(pos + win // 2, n) - jnp.maximum(pos - win // 2, 0)


def _pool_finish(box, u, count, pw_ref, ps_ref, out_ref):
    p = (box / count - u).astype(BF16)
    out_ref[0] = jnp.dot(p, pw_ref[...], preferred_element_type=F32) * ps_ref[...]


def _lane_group_select(shape, per_group):
    lane = lax.broadcasted_iota(jnp.int32, shape, 1) // POOL_GC
    out = per_group[POOL_GROUPS - 1]
    for gi in range(POOL_GROUPS - 2, -1, -1):
        out = jnp.where(lane == gi, per_group[gi], out)
    return out


def _pool2d_kernel(rows, prev_ref, cur_ref, next_ref, m_ref, pw_ref, ps_ref, out_ref):
    j = pl.program_id(1)
    nb = pl.num_programs(1)
    u = cur_ref[0]
    x_win = jnp.concatenate([jnp.where(j > 0, prev_ref[0], 0.0), u, jnp.where(j < nb - 1, next_ref[0], 0.0)], axis=0)
    box = _box_sums(x_win, m_ref)
    t = lax.broadcasted_iota(jnp.int32, u.shape, 0)
    row = j * (POOL_TB // GRID_W) + t // GRID_W
    col = t % GRID_W
    count = _lane_group_select(u.shape, [(_clipped_extent(row, w, rows) * _clipped_extent(col, w, GRID_W)).astype(F32)
                                         for w in POOL_WINDOWS])
    _pool_finish(box, u, count, pw_ref, ps_ref, out_ref)


def _pool1d_kernel(cur_ref, m_ref, pw_ref, ps_ref, out_ref):
    u = cur_ref[0]
    box = _box_sums(u, m_ref)
    t = lax.broadcasted_iota(jnp.int32, u.shape, 0)
    count = _lane_group_select(u.shape, [_clipped_extent(t, w, u.shape[0]).astype(F32) for w in POOL_WINDOWS])
    _pool_finish(box, u, count, pw_ref, ps_ref, out_ref)


def _pool_readout_weight(pool_w):
    return jax.scipy.linalg.block_diag(*[pool_w[g] for g in range(POOL_GROUPS)]).astype(BF16)


def _pool2d(u, rows, pool_w, pool_scale):
    B, T, C = u.shape
    nb = T // POOL_TB
    rb = POOL_TB // GRID_W
    t_out, t_in = np.arange(POOL_TB), np.arange(3 * POOL_TB)
    r_o, c_o = t_out // GRID_W, t_out % GRID_W
    r_i, c_i = t_in // GRID_W - rb, t_in % GRID_W
    m = np.stack([((r_i[None, :] >= r_o[:, None] - w // 2) & (r_i[None, :] < r_o[:, None] + w // 2)
                   & (c_i[None, :] >= c_o[:, None] - w // 2) & (c_i[None, :] < c_o[:, None] + w // 2))
                  for w in POOL_WINDOWS]).astype(np.float32)
    blk = lambda f: pl.BlockSpec((1, POOL_TB, C), f)
    full = lambda shape: pl.BlockSpec(shape, lambda b, j: (0,) * len(shape))
    return pl.pallas_call(
        functools.partial(_pool2d_kernel, rows),
        grid=(B, nb),
        in_specs=[blk(lambda b, j: (b, jnp.maximum(j - 1, 0), 0)), blk(lambda b, j: (b, j, 0)),
                  blk(lambda b, j: (b, jnp.minimum(j + 1, nb - 1), 0)),
                  full((POOL_GROUPS, POOL_TB, 3 * POOL_TB)), full((C, C)), full((1, C))],
        out_specs=blk(lambda b, j: (b, j, 0)),
        out_shape=jax.ShapeDtypeStruct((B, T, C), F32),
        compiler_params=pltpu.CompilerParams(dimension_semantics=("arbitrary", "arbitrary"),
                                             vmem_limit_bytes=48 * 1024 * 1024),
        name="pool2d",
    )(u, u, u, jnp.asarray(m, BF16), _pool_readout_weight(pool_w), pool_scale.reshape(1, C))


def _pool1d(u, pool_w, pool_scale):
    B, T, C = u.shape
    t = np.arange(T)
    m = np.stack([((t[None, :] >= t[:, None] - w // 2) & (t[None, :] < t[:, None] + w // 2))
                  for w in POOL_WINDOWS]).astype(np.float32)
    full = lambda shape: pl.BlockSpec(shape, lambda b: (0,) * len(shape))
    return pl.pallas_call(
        _pool1d_kernel,
        grid=(B,),
        in_specs=[pl.BlockSpec((1, T, C), lambda b: (b, 0, 0)), full((POOL_GROUPS, T, T)), full((C, C)), full((1, C))],
        out_specs=pl.BlockSpec((1, T, C), lambda b: (b, 0, 0)),
        out_shape=jax.ShapeDtypeStruct((B, T, C), F32),
        compiler_params=pltpu.CompilerParams(dimension_semantics=("arbitrary",)),
        name="pool1d",
    )(u, jnp.asarray(m, BF16), _pool_readout_weight(pool_w), pool_scale.reshape(1, C))


FOUR_N1 = 128
FOUR_LB = 2048


def _four_a_kernel(x_ref, fr_ref, fi_ref, yr_ref, yi_ref):
    x = x_ref[0]
    yr_ref[0] = jnp.dot(fr_ref[...], x, precision=HIGHEST, preferred_element_type=F32)
    yi_ref[0] = jnp.dot(fi_ref[...], x, precision=HIGHEST, preferred_element_type=F32)


def _four_b_kernel(yr_ref, yi_ref, fr_ref, fi_ref, twr_ref, twi_ref, cc_ref, ss_ref, w_ref, out_ref):
    c = pl.program_id(1)
    twr = twr_ref[pl.ds(c, 1), :]
    twi = twi_ref[pl.ds(c, 1), :]
    fr, fi = fr_ref[...], fi_ref[...]
    hr = fr * twr - fi * twi
    hi = fr * twi + fi * twr
    yr, yi = yr_ref[0], yi_ref[0]
    dot = lambda a, b: jnp.dot(a, b, precision=HIGHEST, preferred_element_type=F32)
    xr = dot(hr, yr) - dot(hi, yi)
    xi = dot(hr, yi) + dot(hi, yr)
    f = dot(xr, cc_ref[...]) + dot(xi, ss_ref[...])
    out_ref[0] = jnp.dot(f.astype(BF16), w_ref[...], preferred_element_type=F32)


def _four_ctx_kernel(x_ref, fr_ref, fi_ref, cc_ref, ss_ref, w_ref, out_ref):
    x = x_ref[0]
    dot = lambda a, b: jnp.dot(a, b, precision=HIGHEST, preferred_element_type=F32)
    f = dot(dot(fr_ref[...], x), cc_ref[...]) + dot(dot(fi_ref[...], x), ss_ref[...])
    out_ref[0] = jnp.dot(f.astype(BF16), w_ref[...], preferred_element_type=F32)


def _dft_parts(n, scale):
    ang = 2.0 * np.pi * np.outer(np.arange(n), np.arange(n)) / n
    return jnp.asarray(np.cos(ang) * scale, F32), jnp.asarray(-np.sin(ang) * scale, F32)


def _channel_dft():
    ang = 2.0 * np.pi * np.outer(np.arange(FOURIER_HC), np.arange(FOURIER_HC)) / FOURIER_HC
    eye = np.eye(FOURIER_HEADS)
    return jnp.asarray(np.kron(eye, np.cos(ang)), F32), jnp.asarray(np.kron(eye, np.sin(ang)), F32)


def _fourier_readout_weight(fourier_w):
    return jax.scipy.linalg.block_diag(*[fourier_w[h] for h in range(FOURIER_HEADS)]).astype(BF16)


def _fourier_lat(u, fourier_w):
    B, T, C = u.shape
    n1 = FOUR_N1
    assert T == n1 * n1
    scale = 1.0 / np.sqrt(float(T) * FOURIER_HC)
    fr_s, fi_s = _dft_parts(n1, scale)
    fr, fi = _dft_parts(n1, 1.0)
    ang = 2.0 * np.pi * np.outer(np.arange(n1), np.arange(n1)) / T
    twr, twi = jnp.asarray(np.cos(ang), F32), jnp.asarray(-np.sin(ang), F32)
    cc, ss = _channel_dft()
    sq = lambda n: pl.BlockSpec((n, n), lambda b, i: (0, 0))
    y_shape = jax.ShapeDtypeStruct((B, n1, n1 * C), F32)
    a_blk = pl.BlockSpec((1, n1, FOUR_LB), lambda b, i: (b, 0, i))
    yr, yi = pl.pallas_call(
        _four_a_kernel,
        grid=(B, n1 * C // FOUR_LB),
        in_specs=[a_blk, sq(n1), sq(n1)],
        out_specs=[a_blk, a_blk],
        out_shape=[y_shape, y_shape],
        compiler_params=pltpu.CompilerParams(dimension_semantics=("arbitrary", "arbitrary")),
        name="fourier_a",
    )(u.reshape(B, n1, n1 * C), fr_s, fi_s)
    y_blk = pl.BlockSpec((1, n1, C), lambda b, c: (b, c, 0))
    out = pl.pallas_call(
        _four_b_kernel,
        grid=(B, n1),
        in_specs=[y_blk, y_blk, sq(n1), sq(n1), sq(n1), sq(n1), sq(C), sq(C), sq(C)],
        out_specs=pl.BlockSpec((1, n1, C), lambda b, c: (b, 0, c)),
        out_shape=jax.ShapeDtypeStruct((B, n1, n1 * C), F32),
        compiler_params=pltpu.CompilerParams(dimension_semantics=("arbitrary", "arbitrary")),
        name="fourier_b",
    )(yr.reshape(B, T, C), yi.reshape(B, T, C), fr, fi, twr, twi, cc, ss, _fourier_readout_weight(fourier_w))
    return out.reshape(B, T, C)


def _fourier_ctx(u, fourier_w):
    B, T, C = u.shape
    fr, fi = _dft_parts(T, 1.0 / np.sqrt(float(T) * FOURIER_HC))
    cc, ss = _channel_dft()
    sq = lambda n: pl.BlockSpec((n, n), lambda b: (0, 0))
    blk = pl.BlockSpec((1, T, C), lambda b: (b, 0, 0))
    return pl.pallas_call(
        _four_ctx_kernel,
        grid=(B,),
        in_specs=[blk, sq(T), sq(T), sq(C), sq(C), sq(C)],
        out_specs=blk,
        out_shape=jax.ShapeDtypeStruct((B, T, C), F32),
        compiler_params=pltpu.CompilerParams(dimension_semantics=("arbitrary",)),
        name="fourier_ctx",
    )(u, fr, fi, cc, ss, _fourier_readout_weight(fourier_w))


def _token_mixing(h_lat, h_ctx, mod_lat, mod_ctx, rows, ctx_out, norm_g, w_in, shift_w, decay_w0, decay_up, iclr_a0,
                  iclr_up, gate_up, k_k, k_a, r_k, lnx_g, lnx_b, pool_w, pool_scale, fourier_w, w_out):
    ttot = CTX_LEN + h_lat.shape[1]
    w_in_b, w_out_b = w_in.astype(BF16), w_out.astype(BF16)
    prw, ppool_l, pfour_l = _mix_in(h_lat, norm_g, mod_lat[0], mod_lat[1], w_in_b, 1, ttot)
    prw, ppool_c, pfour_c = _mix_in(h_ctx, norm_g, mod_ctx[0], mod_ctx[1], w_in_b, 0, ttot, prw_all=prw)
    lt0, lt1, v, gate, bonus = _rwkv_prep(prw, shift_w, k_k, k_a, r_k, decay_w0, decay_up, iclr_a0, iclr_up, gate_up)
    y_f, y_b = _delta_scan(lt0, lt1, v)
    pool_l = _pool2d(ppool_l, rows, pool_w, pool_scale)
    four_l = _fourier_lat(pfour_l, fourier_w)
    out_lat = _mix_out(y_f, y_b, v, bonus, gate, pool_l, four_l, h_lat, lnx_g, lnx_b, w_out_b, mod_lat[2], 1)
    if not ctx_out:
        return out_lat, h_ctx
    pool_c = _pool1d(ppool_c, pool_w, pool_scale)
    four_c = _fourier_ctx(pfour_c, fourier_w)
    out_ctx = _mix_out(y_f, y_b, v, bonus, gate, pool_c, four_c, h_ctx, lnx_g, lnx_b, w_out_b, mod_ctx[2], 0)
    return out_lat, out_ctx


MOE_TB = 256
MOE_ALIGN = BF16_ROWS
MOE_WIN = MOE_TB + MOE_ALIGN
MOE_EG = 4
MOE_GATHER_BLOCKS = 8
FFN_ROWS = 512


def _ffn_pre_kernel(h_ref, g_ref, sh_ref, sc_ref, rw_ref, rwt_ref, hn_ref, aff_ref, afft_ref):
    x = h_ref[0]
    xn = x * lax.rsqrt(jnp.mean(x * x, axis=-1, keepdims=True) + RMS_EPS) * g_ref[...]
    hn = xn * (1.0 + sc_ref[0]) + sh_ref[0]
    hn_ref[0] = hn.astype(BF16)
    logits = jnp.dot(hn, rw_ref[...], precision=HIGHEST, preferred_element_type=F32)
    ex = jnp.exp(logits - jnp.max(logits, axis=-1, keepdims=True))
    aff_ref[0] = ex / jnp.sum(ex, axis=-1, keepdims=True)
    logits_t = lax.dot_general(rwt_ref[...], hn, (((1,), (1,)), ((), ())), precision=HIGHEST,
                               preferred_element_type=F32)
    ext = jnp.exp(logits_t - jnp.max(logits_t, axis=0, keepdims=True))
    afft_ref[0] = ext / jnp.sum(ext, axis=0, keepdims=True)


def _ffn_pre(h, g, shift, scale, router_w):
    B, T, D = h.shape
    E = N_EXPERTS
    tok = lambda w: pl.BlockSpec((1, MOE_TB, w), lambda b, j: (b, j, 0))
    vec = pl.BlockSpec((1, 1, D), lambda b, j: (b, 0, 0))
    return pl.pallas_call(
        _ffn_pre_kernel,
        grid=(B, T // MOE_TB),
        in_specs=[tok(D), pl.BlockSpec((1, D), lambda b, j: (0, 0)), vec, vec,
                  pl.BlockSpec((D, E), lambda b, j: (0, 0)), pl.BlockSpec((E, D), lambda b, j: (0, 0))],
        out_specs=[tok(D), tok(E), pl.BlockSpec((1, E, MOE_TB), lambda b, j: (b, 0, j))],
        out_shape=[jax.ShapeDtypeStruct((B, T, D), BF16), jax.ShapeDtypeStruct((B, T, E), F32),
                   jax.ShapeDtypeStruct((B, E, T), F32)],
        compiler_params=pltpu.CompilerParams(dimension_semantics=("arbitrary", "arbitrary")),
        name="ffn_pre",
    )(h, g.reshape(1, D), shift, scale, router_w, router_w.T)


def _topk_kernel(cap, afft_ref, tri_ref, ones_ref, pos_ref, start_ref):
    x = afft_ref[0]
    E, T = x.shape
    xi = pltpu.bitcast(x, jnp.int32)
    capf = jnp.float32(cap)

    def count(mask):
        return jnp.sum(jnp.where(mask, 1.0, 0.0), axis=1, keepdims=True)

    thr = jnp.zeros((E, 1), jnp.int32)
    for bit in range(30, -1, -1):
        cand = thr | jnp.int32(1 << bit)
        thr = jnp.where(count(xi >= cand) >= capf, cand, thr)
    need = capf - count(xi > thr)
    tri = tri_ref[...]
    ones = ones_ref[...]
    eq_carry = jnp.zeros((E, MOE_TB), F32)
    sel_carry = jnp.zeros((E, MOE_TB), F32)
    for c in range(T // MOE_TB):
        xc = xi[:, c * MOE_TB:(c + 1) * MOE_TB]
        eq = jnp.where(xc == thr, 1.0, 0.0)
        eqb = eq.astype(BF16)
        eq_rank = eq_carry + jnp.dot(eqb, tri, preferred_element_type=F32) - eq
        eq_carry = eq_carry + jnp.dot(eqb, ones, preferred_element_type=F32)
        sel = jnp.where(jnp.logical_or(xc > thr, jnp.logical_and(xc == thr, eq_rank < need)), 1.0, 0.0)
        selb = sel.astype(BF16)
        slot = sel_carry + jnp.dot(selb, tri, preferred_element_type=F32) - sel
        pos_ref[0, :, c * MOE_TB:(c + 1) * MOE_TB] = jnp.where(sel > 0.0, slot, -1.0).astype(jnp.int32)
        start_ref[0, :, c:c + 1] = sel_carry[:, 0:1].astype(jnp.int32)
        sel_carry = sel_carry + jnp.dot(selb, ones, preferred_element_type=F32)


def _topk_select(afft, cap):
    B, E, T = afft.shape
    nb = T // MOE_TB
    tri = jnp.asarray(np.triu(np.ones((MOE_TB, MOE_TB), np.float32)), BF16)
    ones = jnp.ones((MOE_TB, MOE_TB), BF16)
    sq = pl.BlockSpec((MOE_TB, MOE_TB), lambda b: (0, 0))
    return pl.pallas_call(
        functools.partial(_topk_kernel, cap),
        grid=(B,),
        in_specs=[pl.BlockSpec((1, E, T), lambda b: (b, 0, 0)), sq, sq],
        out_specs=[pl.BlockSpec((1, E, T), lambda b: (b, 0, 0)), pl.BlockSpec((1, E, nb), lambda b: (b, 0, 0))],
        out_shape=[jax.ShapeDtypeStruct((B, E, T), jnp.int32), jax.ShapeDtypeStruct((B, E, nb), jnp.int32)],
        compiler_params=pltpu.CompilerParams(dimension_semantics=("arbitrary",),
                                             vmem_limit_bytes=48 * 1024 * 1024),
        name="topk_select",
    )(afft, tri, ones)


def _slot_onehot(pos_row, start):
    base = pl.multiple_of((start // MOE_ALIGN) * MOE_ALIGN, MOE_ALIGN)
    s_iota = lax.broadcasted_iota(jnp.int32, (MOE_WIN, MOE_TB), 0)
    onehot = jnp.where(pos_row - base == s_iota, 1.0, 0.0).astype(BF16)
    return onehot, base


def _expert_kernel(cap, gb, start_ref, hn_ref, pos_ref, wg_ref, wu_ref, wd_ref, ye_ref, xe_ref):
    b, e, j = pl.program_id(0), pl.program_id(1), pl.program_id(2)
    nb = pl.num_programs(2)

    @pl.when(j == 0)
    def _():
        xe_ref[...] = jnp.zeros_like(xe_ref)

    for q in range(gb):
        tok = slice(q * MOE_TB, (q + 1) * MOE_TB)
        onehot, base = _slot_onehot(pos_ref[0, 0, :, tok], start_ref[((b * N_EXPERTS + e) * nb + j) * gb + q])
        xe_ref[pl.ds(base, MOE_WIN), :] += jnp.dot(onehot, hn_ref[0, tok, :], preferred_element_type=F32)

    @pl.when(j == nb - 1)
    def _():
        rows = min(FFN_ROWS, cap)
        for r0 in range(0, cap, rows):
            xb = xe_ref[r0:r0 + rows, :].astype(BF16)
            gt = jnp.dot(xb, wg_ref[0], preferred_element_type=F32)
            up = jnp.dot(xb, wu_ref[0], preferred_element_type=F32)
            hid = (gt * _sigmoid(gt) * up).astype(BF16)
            ye_ref[0, 0, r0:r0 + rows, :] = jnp.dot(hid, wd_ref[0], preferred_element_type=F32).astype(BF16)
        ye_ref[0, 0, cap:, :] = jnp.zeros((MOE_WIN, ye_ref.shape[-1]), BF16)


def _expert_ffn(hn, pos4, start_flat, wg, wu, wd, cap):
    B, T, D = hn.shape
    E = N_EXPERTS
    capp = cap + MOE_WIN
    gb = min(MOE_GATHER_BLOCKS, T // MOE_TB)
    wspec = pl.BlockSpec((1, D, D), lambda b, e, j, s: (e, 0, 0))
    return pl.pallas_call(
        functools.partial(_expert_kernel, cap, gb),
        grid_spec=pltpu.PrefetchScalarGridSpec(
            num_scalar_prefetch=1,
            grid=(B, E, T // (gb * MOE_TB)),
            in_specs=[pl.BlockSpec((1, gb * MOE_TB, D), lambda b, e, j, s: (b, j, 0)),
                      pl.BlockSpec((1, 1, 1, gb * MOE_TB), lambda b, e, j, s: (b, e, 0, j)),
                      wspec, wspec, wspec],
            out_specs=pl.BlockSpec((1, 1, capp, D), lambda b, e, j, s: (b, e, 0, 0)),
            scratch_shapes=[pltpu.VMEM((capp, D), F32)],
        ),
        out_shape=jax.ShapeDtypeStruct((B, E, capp, D), BF16),
        compiler_params=pltpu.CompilerParams(dimension_semantics=("arbitrary", "arbitrary", "arbitrary"),
                                             vmem_limit_bytes=56 * 1024 * 1024),
        name="expert_ffn",
    )(start_flat, hn, pos4, wg, wu, wd)


def _combine_kernel(eg, start_ref, ye_ref, pos_ref, aff_ref, g2_ref, h_ref, out_ref):
    b, g, j = pl.program_id(0), pl.program_id(1), pl.program_id(2)
    nb = pl.num_programs(2)
    aff = aff_ref[0]
    lane = lax.broadcasted_iota(jnp.int32, aff.shape, 1)
    acc = jnp.zeros(h_ref.shape[1:], F32)
    for el in range(eg):
        e = g * eg + el
        onehot, base = _slot_onehot(pos_ref[0, el], start_ref[(b * N_EXPERTS + e) * nb + j])
        got = lax.dot_general(onehot, ye_ref[0, el, pl.ds(base, MOE_WIN), :], (((0,), (0,)), ((), ())),
                              preferred_element_type=F32)
        gate = jnp.sum(jnp.where(lane == e, aff, 0.0), axis=-1, keepdims=True)
        acc = acc + got * gate
    out_ref[0] = h_ref[0] + g2_ref[0] * acc


def _moe_combine(h, ye, pos4, aff, g2, start_flat):
    B, T, D = h.shape
    E = N_EXPERTS
    capp = ye.shape[2]
    nb = T // MOE_TB
    eg = MOE_EG if nb > 1 else E
    tok = lambda w: pl.BlockSpec((1, MOE_TB, w), lambda b, g, j, s: (b, j, 0))
    return pl.pallas_call(
        functools.partial(_combine_kernel, eg),
        grid_spec=pltpu.PrefetchScalarGridSpec(
            num_scalar_prefetch=1,
            grid=(B, E // eg, nb),
            in_specs=[pl.BlockSpec((1, eg, capp, D), lambda b, g, j, s: (b, g, 0, 0),
                                   pipeline_mode=pl.Buffered(1)),
                      pl.BlockSpec((1, eg, 1, MOE_TB), lambda b, g, j, s: (b, g, 0, j)),
                      tok(E), pl.BlockSpec((1, 1, D), lambda b, g, j, s: (b, 0, 0)), tok(D)],
            out_specs=tok(D),
        ),
        out_shape=jax.ShapeDtypeStruct((B, T, D), F32),
        input_output_aliases={5: 0},
        compiler_params=pltpu.CompilerParams(dimension_semantics=("arbitrary", "arbitrary", "arbitrary"),
                                             vmem_limit_bytes=56 * 1024 * 1024),
        name="moe_combine",
    )(start_flat, ye, pos4, aff, g2, h)


def _moe_block(h, norm_g, shift, scale, g2, router_w, wg, wu, wd):
    B, T, D = h.shape
    cap = CAPACITY_FACTOR * T // N_EXPERTS
    hn, aff, afft = _ffn_pre(h, norm_g, shift, scale, router_w)
    pos, start = _topk_select(afft, cap)
    pos4 = pos.reshape(B, N_EXPERTS, 1, T)
    start_flat = start.reshape(-1)
    ye = _expert_ffn(hn, pos4, start_flat, wg, wu, wd, cap)
    return _moe_combine(h, ye, pos4, aff, g2, start_flat)


def _final_norm_kernel(h_ref, g_ref, out_ref):
    x = h_ref[0]
    out_ref[0] = x * lax.rsqrt(jnp.mean(x * x, axis=-1, keepdims=True) + RMS_EPS) * g_ref[...]


def _final_norm(h, g):
    B, T, D = h.shape
    tok = pl.BlockSpec((1, TBLK, D), lambda b, j: (b, j, 0))
    return pl.pallas_call(
        _final_norm_kernel,
        grid=(B, T // TBLK),
        in_specs=[tok, pl.BlockSpec((1, D), lambda b, j: (0, 0))],
        out_specs=tok,
        out_shape=jax.ShapeDtypeStruct((B, T, D), F32),
        compiler_params=pltpu.CompilerParams(dimension_semantics=("arbitrary", "arbitrary")),
        name="final_norm",
    )(h, g.reshape(1, D))


def kernel(x, c, ctx, c_ctx, ada_w, ada_b, norm_mix_g, norm_ffn_g, w_in, shift_w, decay_w0, decay_up, iclr_a0,
           iclr_up, gate_up, k_k, k_a, r_k, lnx_g, lnx_b, pool_w, pool_scale, fourier_w, w_out, router_w,
           exp_w_gate, exp_w_up, exp_w_down, final_norm_g):
    rows = x.shape[1] // GRID_W
    h_lat, h_ctx = x, ctx
    for l in range(DEPTH):
        ctx_out = l < DEPTH - 1
        B, D = h_lat.shape[0], h_lat.shape[2]
        sh1_l, sc1_l, g1_l, sh2_l, sc2_l, g2_l = [m[:, None, :] for m in _adaln(c, ada_w[l], ada_b[l])]
        sh1_c, sc1_c, g1_c, sh2_c, sc2_c, g2_c = [jnp.broadcast_to(m[None, None, :], (B, 1, D))
                                                  for m in _adaln(c_ctx, ada_w[l], ada_b[l])]
        h_lat, h_ctx = _token_mixing(
            h_lat, h_ctx, (sh1_l, sc1_l, g1_l), (sh1_c, sc1_c, g1_c), rows, ctx_out, norm_mix_g[l],
            w_in[l], shift_w[l], decay_w0[l], decay_up[l], iclr_a0[l], iclr_up[l], gate_up[l],
            k_k[l], k_a[l], r_k[l], lnx_g[l], lnx_b[l], pool_w[l], pool_scale[l], fourier_w[l], w_out[l])
        experts = (exp_w_gate[l].astype(BF16), exp_w_up[l].astype(BF16), exp_w_down[l].astype(BF16))
        h_lat = _moe_block(h_lat, norm_ffn_g[l], sh2_l, sc2_l, g2_l, router_w[l], *experts)
        if ctx_out:
            h_ctx = _moe_block(h_ctx, norm_ffn_g[l], sh2_c, sc2_c, g2_c, router_w[l], *experts)
    return _final_norm(h_lat, final_norm_g)
```

```python
import functools

import numpy as np
import jax
import jax.numpy as jnp
from jax import lax
from jax.experimental import pallas as pl
from jax.experimental.pallas import tpu as pltpu

F32 = jnp.float32
BF16 = jnp.bfloat16
HIGHEST = lax.Precision.HIGHEST

D_MODEL = 1024
DEPTH = 2
GRID_W = 64
CTX_LEN = 256
RWKV_HEAD = 64
RWKV_WIDTH = 512
RWKV_HEADS = 8
DECAY_LORA = 64
ICLR_LORA = 64
GATE_LORA = 128
RWKV_COLS = 3 * RWKV_WIDTH + DECAY_LORA + ICLR_LORA + GATE_LORA
POOL_WIDTH = 256
POOL_WINDOWS = (2, 4, 8, 16)
POOL_GROUPS = 4
POOL_GC = 64
FOURIER_WIDTH = 256
FOURIER_HEADS = 4
FOURIER_HC = 64
IN_COLS = RWKV_COLS + POOL_WIDTH + FOURIER_WIDTH
MIX_SPLITS = (RWKV_COLS, RWKV_COLS + POOL_WIDTH)
N_EXPERTS = 16
CAPACITY_FACTOR = 2
N_MOD = 6
RMS_EPS = 1e-6
LNX_EPS = 64e-5

SUBLANES = 8
LANES = 128
BF16_ROWS = 16

SCAN_TC = 32
SCAN_SUBS = 8
TBLK = SCAN_TC * SCAN_SUBS
SCAN_NSEL = SCAN_TC // 2
SCAN_PARTS = 4
SCAN_KSEL = 2 * SCAN_PARTS * SCAN_TC
SCAN_NOPS = 5
SCAN_LROWS = SCAN_NOPS * RWKV_HEAD
KGROUPS = RWKV_HEAD // SUBLANES
HEAD_PAIRS = RWKV_HEADS // 2
assert TBLK == CTX_LEN and SCAN_PARTS * SCAN_TC == LANES


def _mirror_block(i, nblk):
    return jnp.where(i == 0, 0, nblk - i)


def _split3_exact(x):
    mask = jnp.uint32(0xFFFF0000)
    hi = pltpu.bitcast(pltpu.bitcast(x, jnp.uint32) & mask, F32)
    r1 = x - hi
    mid = pltpu.bitcast(pltpu.bitcast(r1, jnp.uint32) & mask, F32)
    return hi, mid, r1 - mid


def _head_sum(x, bd):
    acc = None
    for part in _split3_exact(x):
        term = jnp.dot(part.astype(BF16), bd, preferred_element_type=F32)
        acc = term if acc is None else acc + term
    return acc


def _head_blockdiag():
    return jnp.asarray(np.kron(np.eye(RWKV_HEADS, dtype=np.float32), np.ones((RWKV_HEAD, RWKV_HEAD), np.float32)), BF16)


def _sigmoid(x):
    return 1.0 / (1.0 + jnp.exp(-x))


def _softplus(x):
    return jnp.maximum(x, 0.0) + jnp.log(1.0 + jnp.exp(-jnp.abs(x)))


def _prep_kernel(p_ref, prev_ref, next_ref, sw_ref, kk_ref, ka_ref, rk_ref, w0_ref, a0_ref, dup_ref, aup_ref,
                 gup_ref, bd_ref, lt0_ref, lt1_ref, v_ref, gate_ref, bonus_ref):
    i = pl.program_id(1)
    nblk = pl.num_programs(1)
    prev_ok = i >= 2
    next_ok = jnp.logical_and(i >= 1, i <= nblk - 2)
    row = lax.broadcasted_iota(jnp.int32, (TBLK, 1), 0)

    def section(c0, c1):
        x = p_ref[0, :, c0:c1]
        prev_row = jnp.where(prev_ok, prev_ref[0, SUBLANES - 1:SUBLANES, c0:c1], 0.0)
        next_row = jnp.where(next_ok, next_ref[0, 0:1, c0:c1], 0.0)
        x_prev = jnp.where(row == 0, prev_row, pltpu.roll(x, 1, 0))
        x_next = jnp.where(row == TBLK - 1, next_row, pltpu.roll(x, TBLK - 1, 0))
        return x_prev * sw_ref[0:1, c0:c1] + x * sw_ref[1:2, c0:c1] + x_next * sw_ref[2:3, c0:c1]

    W = RWKV_WIDTH
    r = section(0, W)
    k = section(W, 2 * W)
    v = section(2 * W, 3 * W)
    xwa = section(3 * W, 3 * W + LANES)
    xg = section(3 * W + LANES, 3 * W + 2 * LANES)
    bd = bd_ref[...]

    kk = k * kk_ref[...]
    kk = kk * lax.rsqrt(_head_sum(kk * kk, bd) + 1e-12)
    v_ref[0] = v
    gate_ref[0] = jnp.dot(_sigmoid(xg), gup_ref[...], precision=HIGHEST, preferred_element_type=F32)
    txwa = jnp.tanh(xwa)
    a_neg = -kk

    def store_parts(lt_ref, d, o, x):
        parts = _split3_exact(x)
        for s in range(SCAN_SUBS):
            slot = s if d == 0 else SCAN_SUBS - 1 - s
            for p in range(3):
                lt_ref[0, slot, o, p * SCAN_TC:(p + 1) * SCAN_TC, :] = (
                    parts[p][s * SCAN_TC:(s + 1) * SCAN_TC, :].astype(BF16))
            lt_ref[0, slot, o, 3 * SCAN_TC:, :] = jnp.zeros((SCAN_TC, W), BF16)

    bonus = None
    for d, lt_ref in ((0, lt0_ref), (1, lt1_ref)):
        lw = jnp.dot(txwa, dup_ref[d], precision=HIGHEST, preferred_element_type=F32)
        w_log = -_softplus(-(w0_ref[d:d + 1, :] + lw)) - 0.5
        decay = jnp.exp(-jnp.exp(w_log))
        a_lr = _sigmoid(a0_ref[d:d + 1, :] + jnp.dot(xwa, aup_ref[d], precision=HIGHEST, preferred_element_type=F32))
        k_dir = k * (1.0 + (a_lr - 1.0) * ka_ref[...])
        bterm = _head_sum(r * k_dir * rk_ref[...], bd)
        bonus = bterm if bonus is None else bonus + bterm
        store_parts(lt_ref, d, 0, decay)
        store_parts(lt_ref, d, 1, a_neg)
        store_parts(lt_ref, d, 2, kk * a_lr)
        store_parts(lt_ref, d, 3, k_dir)
        store_parts(lt_ref, d, 4, r)
    bonus_ref[0] = bonus


def _rwkv_prep(p_all, shift_w, k_k, k_a, r_k, decay_w0, decay_up, iclr_a0, iclr_up, gate_up):
    B, TT, _ = p_all.shape
    nblk = TT // TBLK
    W = RWKV_WIDTH
    zeros = jnp.zeros((2, DECAY_LORA, W), F32)
    dup_pad = jnp.concatenate([decay_up, zeros], axis=1)
    aup_pad = jnp.concatenate([zeros, iclr_up], axis=1)
    bd = _head_blockdiag()
    rows8 = TT // SUBLANES
    full = lambda shape: pl.BlockSpec(shape, lambda b, i: (0,) * len(shape))
    lt_shape = jax.ShapeDtypeStruct((B, nblk * SCAN_SUBS, SCAN_NOPS, LANES, W), BF16)
    tok_shape = jax.ShapeDtypeStruct((B, TT, W), F32)
    tok_spec = pl.BlockSpec((1, TBLK, W), lambda b, i: (b, i, 0))
    return pl.pallas_call(
        _prep_kernel,
        grid=(B, nblk),
        in_specs=[
            pl.BlockSpec((1, TBLK, RWKV_COLS), lambda b, i: (b, i, 0)),
            pl.BlockSpec((1, SUBLANES, RWKV_COLS), lambda b, i: (b, jnp.maximum(i * (TBLK // SUBLANES) - 1, 0), 0)),
            pl.BlockSpec((1, SUBLANES, RWKV_COLS),
                         lambda b, i: (b, jnp.minimum((i + 1) * (TBLK // SUBLANES), rows8 - 1), 0)),
            full((3, RWKV_COLS)), full((1, W)), full((1, W)), full((1, W)), full((2, W)), full((2, W)),
            full((2, LANES, W)), full((2, LANES, W)), full((GATE_LORA, W)), full((W, W)),
        ],
        out_specs=[
            pl.BlockSpec((1, SCAN_SUBS, SCAN_NOPS, LANES, W), lambda b, i: (b, i, 0, 0, 0)),
            pl.BlockSpec((1, SCAN_SUBS, SCAN_NOPS, LANES, W), lambda b, i: (b, _mirror_block(i, nblk), 0, 0, 0)),
            tok_spec, tok_spec, tok_spec,
        ],
        out_shape=[lt_shape, lt_shape, tok_shape, tok_shape, tok_shape],
        compiler_params=pltpu.CompilerParams(
            dimension_semantics=("arbitrary", "arbitrary"),
            vmem_limit_bytes=48 * 1024 * 1024,
        ),
        name="rwkv_prep",
    )(p_all, p_all, p_all, shift_w, k_k.reshape(1, W), k_a.reshape(1, W), r_k.reshape(1, W), decay_w0, iclr_a0,
      dup_pad, aup_pad, gate_up, bd)


def _sublane_allsum(x):
    x = x + pltpu.roll(x, 4, 0)
    x = x + pltpu.roll(x, 2, 0)
    return x + pltpu.roll(x, 1, 0)


def _tree_sum(xs):
    xs = list(xs)
    while len(xs) > 1:
        xs = [xs[i] + xs[i + 1] for i in range(0, len(xs), 2)]
    return xs[0]


def _scan_kernel(lt0_ref, lt1_ref, ltn0_ref, ltn1_ref, sel_ref, vf_ref, vb_ref, yf_ref, yb_ref,
                 e0_ref, e1_ref, l_ref, s_ref):
    i = pl.program_id(1)
    lt_refs = (lt0_ref, lt1_ref)
    ltn_refs = (ltn0_ref, ltn1_ref)
    v_refs = (vf_ref, vb_ref)
    y_refs = (yf_ref, yb_ref)

    def build_l(g, src_ref, idx):
        for o in range(SCAN_NOPS):
            xt = src_ref[0, idx, o].T
            l_ref[g, o * RWKV_HEAD:(o + 1) * RWKV_HEAD, 0:LANES] = xt[0:RWKV_HEAD]
            l_ref[g, o * RWKV_HEAD:(o + 1) * RWKV_HEAD, LANES:2 * LANES] = xt[RWKV_HEAD:2 * RWKV_HEAD]

    def expand(e_ref, n):
        for g in range(2):
            e_ref[g, n] = jnp.dot(l_ref[g], sel_ref[g, n], preferred_element_type=F32)

    @pl.when(i == 0)
    def _():
        s_ref[...] = jnp.zeros_like(s_ref)
        for g in range(2):
            build_l(g, lt_refs[g], 0)
        for n in range(SCAN_NSEL):
            expand(e0_ref, n)

    def scan_sub(sub, e_cur, e_nxt):
        @pl.when(sub < SCAN_SUBS - 1)
        def _():
            for g in range(2):
                build_l(g, lt_refs[g], sub + 1)

        @pl.when(sub == SCAN_SUBS - 1)
        def _():
            for g in range(2):
                build_l(g, ltn_refs[g], 0)

        state = [[s_ref[g, SUBLANES * j:SUBLANES * (j + 1), :] for j in range(KGROUPS)] for g in range(2)]
        for n in range(SCAN_NSEL):
            expand(e_nxt, n)
            for s in range(2):
                t = sub * SCAN_TC + 2 * n + s
                for g in range(2):
                    st = state[g]
                    trow = t if g == 0 else TBLK - 1 - t

                    def col(o, j, g=g, s=s, n=n):
                        r0 = o * RWKV_HEAD + SUBLANES * j
                        return e_cur[g, n, r0:r0 + SUBLANES, s * LANES:(s + 1) * LANES]

                    sa = _sublane_allsum(_tree_sum(st[j] * col(1, j) for j in range(KGROUPS)))
                    v8 = jnp.broadcast_to(v_refs[g][0, pl.ds(trow, 1), :], (SUBLANES, LANES))
                    nst = [st[j] * col(0, j) + col(2, j) * sa + col(3, j) * v8 for j in range(KGROUPS)]
                    y8 = _sublane_allsum(_tree_sum(nst[j] * col(4, j) for j in range(KGROUPS)))
                    y_refs[g][0, pl.ds(trow, 1), :] = y8[0:1, :]
                    state[g] = nst
        for g in range(2):
            for j in range(KGROUPS):
                s_ref[g, SUBLANES * j:SUBLANES * (j + 1), :] = state[g][j]

    def sub_pair(q, carry):
        scan_sub(2 * q, e0_ref, e1_ref)
        scan_sub(2 * q + 1, e1_ref, e0_ref)
        return carry

    lax.fori_loop(0, SCAN_SUBS // 2, sub_pair, 0)


def _scan_selector():
    sel = np.zeros((2, SCAN_NSEL, 2, SCAN_PARTS, SCAN_TC, 2, LANES), np.float32)
    for g in range(2):
        for n in range(SCAN_NSEL):
            for s in range(2):
                step = 2 * n + s if g == 0 else SCAN_TC - 1 - (2 * n + s)
                for j in range(2):
                    sel[g, n, j, :3, step, s, j * RWKV_HEAD:(j + 1) * RWKV_HEAD] = 1.0
    return jnp.asarray(sel.reshape(2, SCAN_NSEL, SCAN_KSEL, 2 * LANES), BF16)


def _delta_scan(lt0, lt1, v_all):
    B, TT, W = v_all.shape
    nblk = TT // TBLK
    nsub = nblk * SCAN_SUBS
    lt_spec = pl.BlockSpec((1, SCAN_SUBS, SCAN_NOPS, LANES, LANES), lambda c, i: (c // HEAD_PAIRS, i, 0, 0, c % HEAD_PAIRS))
    ltn_spec = pl.BlockSpec((1, 1, SCAN_NOPS, LANES, LANES),
                            lambda c, i: (c // HEAD_PAIRS, jnp.minimum((i + 1) * SCAN_SUBS, nsub - 1), 0, 0, c % HEAD_PAIRS))
    f_spec = pl.BlockSpec((1, TBLK, LANES), lambda c, i: (c // HEAD_PAIRS, i, c % HEAD_PAIRS))
    b_spec = pl.BlockSpec((1, TBLK, LANES), lambda c, i: (c // HEAD_PAIRS, _mirror_block(i, nblk), c % HEAD_PAIRS))
    y_shape = jax.ShapeDtypeStruct((B, TT, W), F32)
    return pl.pallas_call(
        _scan_kernel,
        grid=(B * HEAD_PAIRS, nblk),
        in_specs=[lt_spec, lt_spec, ltn_spec, ltn_spec,
                  pl.BlockSpec((2, SCAN_NSEL, SCAN_KSEL, 2 * LANES), lambda c, i: (0, 0, 0, 0)),
                  f_spec, b_spec],
        out_specs=[f_spec, b_spec],
        out_shape=[y_shape, y_shape],
        scratch_shapes=[
            pltpu.VMEM((2, SCAN_NSEL, SCAN_LROWS, 2 * LANES), F32),
            pltpu.VMEM((2, SCAN_NSEL, SCAN_LROWS, 2 * LANES), F32),
            pltpu.VMEM((2, SCAN_LROWS, SCAN_KSEL), BF16),
            pltpu.VMEM((2, RWKV_HEAD, LANES), F32),
        ],
        compiler_params=pltpu.CompilerParams(
            dimension_semantics=("arbitrary", "arbitrary"),
            vmem_limit_bytes=52 * 1024 * 1024,
        ),
        name="delta_scan",
    )(lt0, lt1, lt0, lt1, _scan_selector(), v_all, v_all)


def _heads(t):
    return t.reshape(t.shape[0], t.shape[1], RWKV_HEADS, RWKV_HEAD)


def _rmsnorm(x, g):
    y = x * lax.rsqrt(jnp.mean(x * x, axis=-1, keepdims=True) + RMS_EPS)
    return y * g


def _adaln(cond, ada_w, ada_b):
    return jnp.split(jax.nn.silu(cond) @ ada_w + ada_b, N_MOD, axis=-1)


def _modulate(x, shift, scale):
    return x * (1 + scale) + shift


def _mix_in_kernel(h_ref, g_ref, sh_ref, sc_ref, w_ref, *rest):
    prw_ref, ppool_ref, pfour_ref = rest[-3:]
    x = h_ref[0]
    xn = x * lax.rsqrt(jnp.mean(x * x, axis=-1, keepdims=True) + RMS_EPS) * g_ref[...]
    hn = (xn * (1.0 + sc_ref[0]) + sh_ref[0]).astype(BF16)
    p = jnp.dot(hn, w_ref[...], preferred_element_type=F32)
    prw_ref[0] = p[:, :RWKV_COLS]
    ppool_ref[0] = p[:, RWKV_COLS:RWKV_COLS + POOL_WIDTH]
    pfour_ref[0] = p[:, RWKV_COLS + POOL_WIDTH:]


def _mix_in(h, g, shift, scale, w_in_bf16, blk_off, ttot, prw_all=None):
    B, T, D = h.shape
    tok = lambda w: pl.BlockSpec((1, TBLK, w), lambda b, j: (b, j, 0))
    vec = pl.BlockSpec((1, 1, D), lambda b, j: (b, 0, 0))
    in_specs = [tok(D), pl.BlockSpec((1, D), lambda b, j: (0, 0)), vec, vec,
                pl.BlockSpec((D, IN_COLS), lambda b, j: (0, 0))]
    args = [h, g.reshape(1, D), shift, scale, w_in_bf16]
    aliases = {}
    if prw_all is not None:
        in_specs.append(pl.BlockSpec(memory_space=pl.ANY))
        args.append(prw_all)
        aliases = {5: 0}
    return pl.pallas_call(
        _mix_in_kernel,
        grid=(B, T // TBLK),
        in_specs=in_specs,
        out_specs=[pl.BlockSpec((1, TBLK, RWKV_COLS), lambda b, j: (b, j + blk_off, 0)),
                   tok(POOL_WIDTH), tok(FOURIER_WIDTH)],
        out_shape=[jax.ShapeDtypeStruct((B, ttot, RWKV_COLS), F32), jax.ShapeDtypeStruct((B, T, POOL_WIDTH), F32),
                   jax.ShapeDtypeStruct((B, T, FOURIER_WIDTH), F32)],
        input_output_aliases=aliases,
        compiler_params=pltpu.CompilerParams(dimension_semantics=("arbitrary", "arbitrary"),
                                             vmem_limit_bytes=48 * 1024 * 1024),
        name="mix_in",
    )(*args)


def _mix_out_kernel(yf_ref, yb_ref, v_ref, bonus_ref, gate_ref, pool_ref, four_ref, h_ref, lg_ref, lb_ref, bd_ref,
                    wout_ref, g1_ref, out_ref):
    y = yf_ref[0] + yb_ref[0]
    bd = bd_ref[...]
    inv_n = 1.0 / RWKV_HEAD
    mu = _head_sum(y, bd) * inv_n
    dev = y - mu
    var = _head_sum(dev * dev, bd) * inv_n
    yn = dev * lax.rsqrt(var + LNX_EPS) * lg_ref[...] + lb_ref[...]
    rw = (yn + bonus_ref[0] * v_ref[0]) * gate_ref[0]
    cat = jnp.concatenate([rw, pool_ref[0], four_ref[0]], axis=-1).astype(BF16)
    mix = jnp.dot(cat, wout_ref[...], preferred_element_type=F32)
    out_ref[0] = h_ref[0] + g1_ref[0] * mix


def _mix_out(y_f, y_b, v, bonus, gate, pool_out, four_out, h, lnx_g, lnx_b, w_out_bf16, g1, blk_off):
    B, T, D = h.shape
    W = RWKV_WIDTH
    tok = lambda w: pl.BlockSpec((1, TBLK, w), lambda b, j: (b, j, 0))
    rw_tok = pl.BlockSpec((1, TBLK, W), lambda b, j: (b, j + blk_off, 0))
    full = lambda shape: pl.BlockSpec(shape, lambda b, j: (0,) * len(shape))
    bd = _head_blockdiag()
    return pl.pallas_call(
        _mix_out_kernel,
        grid=(B, T // TBLK),
        in_specs=[rw_tok] * 5 + [tok(POOL_WIDTH), tok(FOURIER_WIDTH), tok(D), full((1, W)), full((1, W)),
                                 full((W, W)), full((D, D)), pl.BlockSpec((1, 1, D), lambda b, j: (b, 0, 0))],
        out_specs=tok(D),
        out_shape=jax.ShapeDtypeStruct((B, T, D), F32),
        compiler_params=pltpu.CompilerParams(dimension_semantics=("arbitrary", "arbitrary"),
                                             vmem_limit_bytes=48 * 1024 * 1024),
        name="mix_out",
    )(y_f, y_b, v, bonus, gate, pool_out, four_out, h, lnx_g.reshape(1, W), lnx_b.reshape(1, W), bd, w_out_bf16, g1)


POOL_TB = 512
assert POOL_TB // GRID_W >= max(POOL_WINDOWS) // 2


def _box_sums(x_win, m_ref):
    parts = [p.astype(BF16) for p in _split3_exact(x_win)]
    cols = []
    for gi in range(POOL_GROUPS):
        sl = slice(gi * POOL_GC, (gi + 1) * POOL_GC)
        acc = None
        for p in parts:
            term = jnp.dot(m_ref[gi], p[:, sl], preferred_element_type=F32)
            acc = term if acc is None else acc + term
        cols.append(acc)
    return jnp.concatenate(cols, axis=-1)


def _clipped_extent(pos, win, n):
    return jnp.minimum(pos + win // 2, n) - jnp.maximum(pos - win // 2, 0)


def _pool_finish(box, u, count, pw_ref, ps_ref, out_ref):
    p = (box / count - u).astype(BF16)
    out_ref[0] = jnp.dot(p, pw_ref[...], preferred_element_type=F32) * ps_ref[...]


def _lane_group_select(shape, per_group):
    lane = lax.broadcasted_iota(jnp.int32, shape, 1) // POOL_GC
    out = per_group[POOL_GROUPS - 1]
    for gi in range(POOL_GROUPS - 2, -1, -1):
        out = jnp.where(lane == gi, per_group[gi], out)
    return out


def _pool2d_kernel(rows, prev_ref, cur_ref, next_ref, m_ref, pw_ref, ps_ref, out_ref):
    j = pl.program_id(1)
    nb = pl.num_programs(1)
    u = cur_ref[0]
    x_win = jnp.concatenate([jnp.where(j > 0, prev_ref[0], 0.0), u, jnp.where(j < nb - 1, next_ref[0], 0.0)], axis=0)
    box = _box_sums(x_win, m_ref)
    t = lax.broadcasted_iota(jnp.int32, u.shape, 0)
    row = j * (POOL_TB // GRID_W) + t // GRID_W
    col = t % GRID_W
    count = _lane_group_select(u.shape, [(_clipped_extent(row, w, rows) * _clipped_extent(col, w, GRID_W)).astype(F32)
                                         for w in POOL_WINDOWS])
    _pool_finish(box, u, count, pw_ref, ps_ref, out_ref)


def _pool1d_kernel(cur_ref, m_ref, pw_ref, ps_ref, out_ref):
    u = cur_ref[0]
    box = _box_sums(u, m_ref)
    t = lax.broadcasted_iota(jnp.int32, u.shape, 0)
    count = _lane_group_select(u.shape, [_clipped_extent(t, w, u.shape[0]).astype(F32) for w in POOL_WINDOWS])
    _pool_finish(box, u, count, pw_ref, ps_ref, out_ref)


def _pool_readout_weight(pool_w):
    return jax.scipy.linalg.block_diag(*[pool_w[g] for g in range(POOL_GROUPS)]).astype(BF16)


def _pool2d(u, rows, pool_w, pool_scale):
    B, T, C = u.shape
    nb = T // POOL_TB
    rb = POOL_TB // GRID_W
    t_out, t_in = np.arange(POOL_TB), np.arange(3 * POOL_TB)
    r_o, c_o = t_out // GRID_W, t_out % GRID_W
    r_i, c_i = t_in // GRID_W - rb, t_in % GRID_W
    m = np.stack([((r_i[None, :] >= r_o[:, None] - w // 2) & (r_i[None, :] < r_o[:, None] + w // 2)
                   & (c_i[None, :] >= c_o[:, None] - w // 2) & (c_i[None, :] < c_o[:, None] + w // 2))
                  for w in POOL_WINDOWS]).astype(np.float32)
    blk = lambda f: pl.BlockSpec((1, POOL_TB, C), f)
    full = lambda shape: pl.BlockSpec(shape, lambda b, j: (0,) * len(shape))
    return pl.pallas_call(
        functools.partial(_pool2d_kernel, rows),
        grid=(B, nb),
        in_specs=[blk(lambda b, j: (b, jnp.maximum(j - 1, 0), 0)), blk(lambda b, j: (b, j, 0)),
                  blk(lambda b, j: (b, jnp.minimum(j + 1, nb - 1), 0)),
                  full((POOL_GROUPS, POOL_TB, 3 * POOL_TB)), full((C, C)), full((1, C))],
        out_specs=blk(lambda b, j: (b, j, 0)),
        out_shape=jax.ShapeDtypeStruct((B, T, C), F32),
        compiler_params=pltpu.CompilerParams(dimension_semantics=("arbitrary", "arbitrary"),
                                             vmem_limit_bytes=48 * 1024 * 1024),
        name="pool2d",
    )(u, u, u, jnp.asarray(m, BF16), _pool_readout_weight(pool_w), pool_scale.reshape(1, C))


def _pool1d(u, pool_w, pool_scale):
    B, T, C = u.shape
    t = np.arange(T)
    m = np.stack([((t[None, :] >= t[:, None] - w // 2) & (t[None, :] < t[:, None] + w // 2))
                  for w in POOL_WINDOWS]).astype(np.float32)
    full = lambda shape: pl.BlockSpec(shape, lambda b: (0,) * len(shape))
    return pl.pallas_call(
        _pool1d_kernel,
        grid=(B,),
        in_specs=[pl.BlockSpec((1, T, C), lambda b: (b, 0, 0)), full((POOL_GROUPS, T, T)), full((C, C)), full((1, C))],
        out_specs=pl.BlockSpec((1, T, C), lambda b: (b, 0, 0)),
        out_shape=jax.ShapeDtypeStruct((B, T, C), F32),
        compiler_params=pltpu.CompilerParams(dimension_semantics=("arbitrary",)),
        name="pool1d",
    )(u, jnp.asarray(m, BF16), _pool_readout_weight(pool_w), pool_scale.reshape(1, C))


FOUR_N1 = 128
FOUR_LB = 2048


def _four_a_kernel(x_ref, fr_ref, fi_ref, yr_ref, yi_ref):
    x = x_ref[0]
    yr_ref[0] = jnp.dot(fr_ref[...], x, precision=HIGHEST, preferred_element_type=F32)
    yi_ref[0] = jnp.dot(fi_ref[...], x, precision=HIGHEST, preferred_element_type=F32)


def _four_b_kernel(yr_ref, yi_ref, fr_ref, fi_ref, twr_ref, twi_ref, cc_ref, ss_ref, w_ref, out_ref):
    c = pl.program_id(1)
    twr = twr_ref[pl.ds(c, 1), :]
    twi = twi_ref[pl.ds(c, 1), :]
    fr, fi = fr_ref[...], fi_ref[...]
    hr = fr * twr - fi * twi
    hi = fr * twi + fi * twr
    yr, yi = yr_ref[0], yi_ref[0]
    dot = lambda a, b: jnp.dot(a, b, precision=HIGHEST, preferred_element_type=F32)
    xr = dot(hr, yr) - dot(hi, yi)
    xi = dot(hr, yi) + dot(hi, yr)
    f = dot(xr, cc_ref[...]) + dot(xi, ss_ref[...])
    out_ref[0] = jnp.dot(f.astype(BF16), w_ref[...], preferred_element_type=F32)


def _four_ctx_kernel(x_ref, fr_ref, fi_ref, cc_ref, ss_ref, w_ref, out_ref):
    x = x_ref[0]
    dot = lambda a, b: jnp.dot(a, b, precision=HIGHEST, preferred_element_type=F32)
    f = dot(dot(fr_ref[...], x), cc_ref[...]) + dot(dot(fi_ref[...], x), ss_ref[...])
    out_ref[0] = jnp.dot(f.astype(BF16), w_ref[...], preferred_element_type=F32)


def _dft_parts(n, scale):
    ang = 2.0 * np.pi * np.outer(np.arange(n), np.arange(n)) / n
    return jnp.asarray(np.cos(ang) * scale, F32), jnp.asarray(-np.sin(ang) * scale, F32)


def _channel_dft():
    ang = 2.0 * np.pi * np.outer(np.arange(FOURIER_HC), np.arange(FOURIER_HC)) / FOURIER_HC
    eye = np.eye(FOURIER_HEADS)
    return jnp.asarray(np.kron(eye, np.cos(ang)), F32), jnp.asarray(np.kron(eye, np.sin(ang)), F32)


def _fourier_readout_weight(fourier_w):
    return jax.scipy.linalg.block_diag(*[fourier_w[h] for h in range(FOURIER_HEADS)]).astype(BF16)


def _fourier_lat(u, fourier_w):
    B, T, C = u.shape
    n1 = FOUR_N1
    assert T == n1 * n1
    scale = 1.0 / np.sqrt(float(T) * FOURIER_HC)
    fr_s, fi_s = _dft_parts(n1, scale)
    fr, fi = _dft_parts(n1, 1.0)
    ang = 2.0 * np.pi * np.outer(np.arange(n1), np.arange(n1)) / T
    twr, twi = jnp.asarray(np.cos(ang), F32), jnp.asarray(-np.sin(ang), F32)
    cc, ss = _channel_dft()
    sq = lambda n: pl.BlockSpec((n, n), lambda b, i: (0, 0))
    y_shape = jax.ShapeDtypeStruct((B, n1, n1 * C), F32)
    a_blk = pl.BlockSpec((1, n1, FOUR_LB), lambda b, i: (b, 0, i))
    yr, yi = pl.pallas_call(
        _four_a_kernel,
        grid=(B, n1 * C // FOUR_LB),
        in_specs=[a_blk, sq(n1), sq(n1)],
        out_specs=[a_blk, a_blk],
        out_shape=[y_shape, y_shape],
        compiler_params=pltpu.CompilerParams(dimension_semantics=("arbitrary", "arbitrary")),
        name="fourier_a",
    )(u.reshape(B, n1, n1 * C), fr_s, fi_s)
    y_blk = pl.BlockSpec((1, n1, C), lambda b, c: (b, c, 0))
    out = pl.pallas_call(
        _four_b_kernel,
        grid=(B, n1),
        in_specs=[y_blk, y_blk, sq(n1), sq(n1), sq(n1), sq(n1), sq(C), sq(C), sq(C)],
        out_specs=pl.BlockSpec((1, n1, C), lambda b, c: (b, 0, c)),
        out_shape=jax.ShapeDtypeStruct((B, n1, n1 * C), F32),
        compiler_params=pltpu.CompilerParams(dimension_semantics=("arbitrary", "arbitrary")),
        name="fourier_b",
    )(yr.reshape(B, T, C), yi.reshape(B, T, C), fr, fi, twr, twi, cc, ss, _fourier_readout_weight(fourier_w))
    return out.reshape(B, T, C)


def _fourier_ctx(u, fourier_w):
    B, T, C = u.shape
    fr, fi = _dft_parts(T, 1.0 / np.sqrt(float(T) * FOURIER_HC))
    cc, ss = _channel_dft()
    sq = lambda n: pl.BlockSpec((n, n), lambda b: (0, 0))
    blk = pl.BlockSpec((1, T, C), lambda b: (b, 0, 0))
    return pl.pallas_call(
        _four_ctx_kernel,
        grid=(B,),
        in_specs=[blk, sq(T), sq(T), sq(C), sq(C), sq(C)],
        out_specs=blk,
        out_shape=jax.ShapeDtypeStruct((B, T, C), F32),
        compiler_params=pltpu.CompilerParams(dimension_semantics=("arbitrary",)),
        name="fourier_ctx",
    )(u, fr, fi, cc, ss, _fourier_readout_weight(fourier_w))


def _token_mixing(h_lat, h_ctx, mod_lat, mod_ctx, rows, ctx_out, norm_g, w_in, shift_w, decay_w0, decay_up, iclr_a0,
                  iclr_up, gate_up, k_k, k_a, r_k, lnx_g, lnx_b, pool_w, pool_scale, fourier_w, w_out):
    ttot = CTX_LEN + h_lat.shape[1]
    w_in_b, w_out_b = w_in.astype(BF16), w_out.astype(BF16)
    prw, ppool_l, pfour_l = _mix_in(h_lat, norm_g, mod_lat[0], mod_lat[1], w_in_b, 1, ttot)
    prw, ppool_c, pfour_c = _mix_in(h_ctx, norm_g, mod_ctx[0], mod_ctx[1], w_in_b, 0, ttot, prw_all=prw)
    lt0, lt1, v, gate, bonus = _rwkv_prep(prw, shift_w, k_k, k_a, r_k, decay_w0, decay_up, iclr_a0, iclr_up, gate_up)
    y_f, y_b = _delta_scan(lt0, lt1, v)
    pool_l = _pool2d(ppool_l, rows, pool_w, pool_scale)
    four_l = _fourier_lat(pfour_l, fourier_w)
    out_lat = _mix_out(y_f, y_b, v, bonus, gate, pool_l, four_l, h_lat, lnx_g, lnx_b, w_out_b, mod_lat[2], 1)
    if not ctx_out:
        return out_lat, h_ctx
    pool_c = _pool1d(ppool_c, pool_w, pool_scale)
    four_c = _fourier_ctx(pfour_c, fourier_w)
    out_ctx = _mix_out(y_f, y_b, v, bonus, gate, pool_c, four_c, h_ctx, lnx_g, lnx_b, w_out_b, mod_ctx[2], 0)
    return out_lat, out_ctx


MOE_TB = 256
MOE_ALIGN = BF16_ROWS
MOE_WIN = MOE_TB + MOE_ALIGN
MOE_EG = 4
MOE_GATHER_BLOCKS = 8
FFN_ROWS = 512


def _ffn_pre_kernel(h_ref, g_ref, sh_ref, sc_ref, rw_ref, rwt_ref, hn_ref, aff_ref, afft_ref):
    x = h_ref[0]
    xn = x * lax.rsqrt(jnp.mean(x * x, axis=-1, keepdims=True) + RMS_EPS) * g_ref[...]
    hn = xn * (1.0 + sc_ref[0]) + sh_ref[0]
    hn_ref[0] = hn.astype(BF16)
    logits = jnp.dot(hn, rw_ref[...], precision=HIGHEST, preferred_element_type=F32)
    ex = jnp.exp(logits - jnp.max(logits, axis=-1, keepdims=True))
    aff_ref[0] = ex / jnp.sum(ex, axis=-1, keepdims=True)
    logits_t = lax.dot_general(rwt_ref[...], hn, (((1,), (1,)), ((), ())), precision=HIGHEST,
                               preferred_element_type=F32)
    ext = jnp.exp(logits_t - jnp.max(logits_t, axis=0, keepdims=True))
    afft_ref[0] = ext / jnp.sum(ext, axis=0, keepdims=True)


def _ffn_pre(h, g, shift, scale, router_w):
    B, T, D = h.shape
    E = N_EXPERTS
    tok = lambda w: pl.BlockSpec((1, MOE_TB, w), lambda b, j: (b, j, 0))
    vec = pl.BlockSpec((1, 1, D), lambda b, j: (b, 0, 0))
    return pl.pallas_call(
        _ffn_pre_kernel,
        grid=(B, T // MOE_TB),
        in_specs=[tok(D), pl.BlockSpec((1, D), lambda b, j: (0, 0)), vec, vec,
                  pl.BlockSpec((D, E), lambda b, j: (0, 0)), pl.BlockSpec((E, D), lambda b, j: (0, 0))],
        out_specs=[tok(D), tok(E), pl.BlockSpec((1, E, MOE_TB), lambda b, j: (b, 0, j))],
        out_shape=[jax.ShapeDtypeStruct((B, T, D), BF16), jax.ShapeDtypeStruct((B, T, E), F32),
                   jax.ShapeDtypeStruct((B, E, T), F32)],
        compiler_params=pltpu.CompilerParams(dimension_semantics=("arbitrary", "arbitrary")),
        name="ffn_pre",
    )(h, g.reshape(1, D), shift, scale, router_w, router_w.T)


def _topk_kernel(cap, afft_ref, tri_ref, ones_ref, pos_ref, start_ref):
    x = afft_ref[0]
    E, T = x.shape
    xi = pltpu.bitcast(x, jnp.int32)
    capf = jnp.float32(cap)

    def count(mask):
        return jnp.sum(jnp.where(mask, 1.0, 0.0), axis=1, keepdims=True)

    thr = jnp.zeros((E, 1), jnp.int32)
    for bit in range(30, -1, -1):
        cand = thr | jnp.int32(1 << bit)
        thr = jnp.where(count(xi >= cand) >= capf, cand, thr)
    need = capf - count(xi > thr)
    tri = tri_ref[...]
    ones = ones_ref[...]
    eq_carry = jnp.zeros((E, MOE_TB), F32)
    sel_carry = jnp.zeros((E, MOE_TB), F32)
    for c in range(T // MOE_TB):
        xc = xi[:, c * MOE_TB:(c + 1) * MOE_TB]
        eq = jnp.where(xc == thr, 1.0, 0.0)
        eqb = eq.astype(BF16)
        eq_rank = eq_carry + jnp.dot(eqb, tri, preferred_element_type=F32) - eq
        eq_carry = eq_carry + jnp.dot(eqb, ones, preferred_element_type=F32)
        sel = jnp.where(jnp.logical_or(xc > thr, jnp.logical_and(xc == thr, eq_rank < need)), 1.0, 0.0)
        selb = sel.astype(BF16)
        slot = sel_carry + jnp.dot(selb, tri, preferred_element_type=F32) - sel
        pos_ref[0, :, c * MOE_TB:(c + 1) * MOE_TB] = jnp.where(sel > 0.0, slot, -1.0).astype(jnp.int32)
        start_ref[0, :, c:c + 1] = sel_carry[:, 0:1].astype(jnp.int32)
        sel_carry = sel_carry + jnp.dot(selb, ones, preferred_element_type=F32)


def _topk_select(afft, cap):
    B, E, T = afft.shape
    nb = T // MOE_TB
    tri = jnp.asarray(np.triu(np.ones((MOE_TB, MOE_TB), np.float32)), BF16)
    ones = jnp.ones((MOE_TB, MOE_TB), BF16)
    sq = pl.BlockSpec((MOE_TB, MOE_TB), lambda b: (0, 0))
    return pl.pallas_call(
        functools.partial(_topk_kernel, cap),
        grid=(B,),
        in_specs=[pl.BlockSpec((1, E, T), lambda b: (b, 0, 0)), sq, sq],
        out_specs=[pl.BlockSpec((1, E, T), lambda b: (b, 0, 0)), pl.BlockSpec((1, E, nb), lambda b: (b, 0, 0))],
        out_shape=[jax.ShapeDtypeStruct((B, E, T), jnp.int32), jax.ShapeDtypeStruct((B, E, nb), jnp.int32)],
        compiler_params=pltpu.CompilerParams(dimension_semantics=("arbitrary",),
                                             vmem_limit_bytes=48 * 1024 * 1024),
        name="topk_select",
    )(afft, tri, ones)


def _slot_onehot(pos_row, start):
    base = pl.multiple_of((start // MOE_ALIGN) * MOE_ALIGN, MOE_ALIGN)
    s_iota = lax.broadcasted_iota(jnp.int32, (MOE_WIN, MOE_TB), 0)
    onehot = jnp.where(pos_row - base == s_iota, 1.0, 0.0).astype(BF16)
    return onehot, base


def _expert_kernel(cap, gb, start_ref, hn_ref, pos_ref, wg_ref, wu_ref, wd_ref, ye_ref, xe_ref):
    b, e, j = pl.program_id(0), pl.program_id(1), pl.program_id(2)
    nb = pl.num_programs(2)

    @pl.when(j == 0)
    def _():
        xe_ref[...] = jnp.zeros_like(xe_ref)

    for q in range(gb):
        tok = slice(q * MOE_TB, (q + 1) * MOE_TB)
        onehot, base = _slot_onehot(pos_ref[0, 0, :, tok], start_ref[((b * N_EXPERTS + e) * nb + j) * gb + q])
        xe_ref[pl.ds(base, MOE_WIN), :] += jnp.dot(onehot, hn_ref[0, tok, :], preferred_element_type=F32)

    @pl.when(j == nb - 1)
    def _():
        rows = min(FFN_ROWS, cap)
        for r0 in range(0, cap, rows):
            xb = xe_ref[r0:r0 + rows, :].astype(BF16)
            gt = jnp.dot(xb, wg_ref[0], preferred_element_type=F32)
            up = jnp.dot(xb, wu_ref[0], preferred_element_type=F32)
            hid = (gt * _sigmoid(gt) * up).astype(BF16)
            ye_ref[0, 0, r0:r0 + rows, :] = jnp.dot(hid, wd_ref[0], preferred_element_type=F32).astype(BF16)
        ye_ref[0, 0, cap:, :] = jnp.zeros((MOE_WIN, ye_ref.shape[-1]), BF16)


def _expert_ffn(hn, pos4, start_flat, wg, wu, wd, cap):
    B, T, D = hn.shape
    E = N_EXPERTS
    capp = cap + MOE_WIN
    gb = min(MOE_GATHER_BLOCKS, T // MOE_TB)
    wspec = pl.BlockSpec((1, D, D), lambda b, e, j, s: (e, 0, 0))
    return pl.pallas_call(
        functools.partial(_expert_kernel, cap, gb),
        grid_spec=pltpu.PrefetchScalarGridSpec(
            num_scalar_prefetch=1,
            grid=(B, E, T // (gb * MOE_TB)),
            in_specs=[pl.BlockSpec((1, gb * MOE_TB, D), lambda b, e, j, s: (b, j, 0)),
                      pl.BlockSpec((1, 1, 1, gb * MOE_TB), lambda b, e, j, s: (b, e, 0, j)),
                      wspec, wspec, wspec],
            out_specs=pl.BlockSpec((1, 1, capp, D), lambda b, e, j, s: (b, e, 0, 0)),
            scratch_shapes=[pltpu.VMEM((capp, D), F32)],
        ),
        out_shape=jax.ShapeDtypeStruct((B, E, capp, D), BF16),
        compiler_params=pltpu.CompilerParams(dimension_semantics=("arbitrary", "arbitrary", "arbitrary"),
                                             vmem_limit_bytes=56 * 1024 * 1024),
        name="expert_ffn",
    )(start_flat, hn, pos4, wg, wu, wd)


def _combine_kernel(eg, start_ref, ye_ref, pos_ref, aff_ref, g2_ref, h_ref, out_ref):
    b, g, j = pl.program_id(0), pl.program_id(1), pl.program_id(2)
    nb = pl.num_programs(2)
    aff = aff_ref[0]
    lane = lax.broadcasted_iota(jnp.int32, aff.shape, 1)
    acc = jnp.zeros(h_ref.shape[1:], F32)
    for el in range(eg):
        e = g * eg + el
        onehot, base = _slot_onehot(pos_ref[0, el], start_ref[(b * N_EXPERTS + e) * nb + j])
        got = lax.dot_general(onehot, ye_ref[0, el, pl.ds(base, MOE_WIN), :], (((0,), (0,)), ((), ())),
                              preferred_element_type=F32)
        gate = jnp.sum(jnp.where(lane == e, aff, 0.0), axis=-1, keepdims=True)
        acc = acc + got * gate
    out_ref[0] = h_ref[0] + g2_ref[0] * acc


def _moe_combine(h, ye, pos4, aff, g2, start_flat):
    B, T, D = h.shape
    E = N_EXPERTS
    capp = ye.shape[2]
    nb = T // MOE_TB
    eg = MOE_EG if nb > 1 else E
    tok = lambda w: pl.BlockSpec((1, MOE_TB, w), lambda b, g, j, s: (b, j, 0))
    return pl.pallas_call(
        functools.partial(_combine_kernel, eg),
        grid_spec=pltpu.PrefetchScalarGridSpec(
            num_scalar_prefetch=1,
            grid=(B, E // eg, nb),
            in_specs=[pl.BlockSpec((1, eg, capp, D), lambda b, g, j, s: (b, g, 0, 0),
                                   pipeline_mode=pl.Buffered(1)),
                      pl.BlockSpec((1, eg, 1, MOE_TB), lambda b, g, j, s: (b, g, 0, j)),
                      tok(E), pl.BlockSpec((1, 1, D), lambda b, g, j, s: (b, 0, 0)), tok(D)],
            out_specs=tok(D),
        ),
        out_shape=jax.ShapeDtypeStruct((B, T, D), F32),
        input_output_aliases={5: 0},
        compiler_params=pltpu.CompilerParams(dimension_semantics=("arbitrary", "arbitrary", "arbitrary"),
                                             vmem_limit_bytes=56 * 1024 * 1024),
        name="moe_combine",
    )(start_flat, ye, pos4, aff, g2, h)


def _moe_block(h, norm_g, shift, scale, g2, router_w, wg, wu, wd):
    B, T, D = h.shape
    cap = CAPACITY_FACTOR * T // N_EXPERTS
    hn, aff, afft = _ffn_pre(h, norm_g, shift, scale, router_w)
    pos, start = _topk_select(afft, cap)
    pos4 = pos.reshape(B, N_EXPERTS, 1, T)
    start_flat = start.reshape(-1)
    ye = _expert_ffn(hn, pos4, start_flat, wg, wu, wd, cap)
    return _moe_combine(h, ye, pos4, aff, g2, start_flat)


ADALN_NB = 512


def _adaln_kernel(c_ref, w_ref, b_ref, out_ref):
    s = c_ref[...]
    s = (s * _sigmoid(s)).astype(BF16)
    out_ref[...] = jnp.dot(s, w_ref[...].astype(BF16), preferred_element_type=F32) + b_ref[...]


def _adaln_rows(cond_rows, ada_w, ada_b):
    R, D = cond_rows.shape
    N = ada_w.shape[1]
    return pl.pallas_call(
        _adaln_kernel,
        grid=(N // ADALN_NB,),
        in_specs=[pl.BlockSpec((R, D), lambda n: (0, 0)), pl.BlockSpec((D, ADALN_NB), lambda n: (0, n)),
                  pl.BlockSpec((1, ADALN_NB), lambda n: (0, n))],
        out_specs=pl.BlockSpec((R, ADALN_NB), lambda n: (0, n)),
        out_shape=jax.ShapeDtypeStruct((R, N), F32),
        compiler_params=pltpu.CompilerParams(dimension_semantics=("arbitrary",)),
        name="adaln",
    )(cond_rows, ada_w, ada_b.reshape(1, N))


def _final_norm_kernel(h_ref, g_ref, out_ref):
    x = h_ref[0]
    out_ref[0] = x * lax.rsqrt(jnp.mean(x * x, axis=-1, keepdims=True) + RMS_EPS) * g_ref[...]


def _final_norm(h, g):
    B, T, D = h.shape
    tok = pl.BlockSpec((1, TBLK, D), lambda b, j: (b, j, 0))
    return pl.pallas_call(
        _final_norm_kernel,
        grid=(B, T // TBLK),
        in_specs=[tok, pl.BlockSpec((1, D), lambda b, j: (0, 0))],
        out_specs=tok,
        out_shape=jax.ShapeDtypeStruct((B, T, D), F32),
        compiler_params=pltpu.CompilerParams(dimension_semantics=("arbitrary", "arbitrary")),
        name="final_norm",
    )(h, g.reshape(1, D))


def kernel(x, c, ctx, c_ctx, ada_w, ada_b, norm_mix_g, norm_ffn_g, w_in, shift_w, decay_w0, decay_up, iclr_a0,
           iclr_up, gate_up, k_k, k_a, r_k, lnx_g, lnx_b, pool_w, pool_scale, fourier_w, w_out, router_w,
           exp_w_gate, exp_w_up, exp_w_down, final_norm_g):
    rows = x.shape[1] // GRID_W
    h_lat, h_ctx = x, ctx
    pad_rows = SUBLANES - (x.shape[0] + 1) % SUBLANES
    cond_rows = jnp.concatenate([c, c_ctx[None, :], jnp.zeros((pad_rows, c.shape[1]), F32)], axis=0)
    for l in range(DEPTH):
        ctx_out = l < DEPTH - 1
        B, D = h_lat.shape[0], h_lat.shape[2]
        mods = _adaln_rows(cond_rows, ada_w[l], ada_b[l])
        sh1_l, sc1_l, g1_l, sh2_l, sc2_l, g2_l = [m[:, None, :] for m in jnp.split(mods[:B], N_MOD, axis=-1)]
        sh1_c, sc1_c, g1_c, sh2_c, sc2_c, g2_c = [jnp.broadcast_to(m[None, None, :], (B, 1, D))
                                                  for m in jnp.split(mods[B], N_MOD, axis=-1)]
        h_lat, h_ctx = _token_mixing(
            h_lat, h_ctx, (sh1_l, sc1_l, g1_l), (sh1_c, sc1_c, g1_c), rows, ctx_out, norm_mix_g[l],
            w_in[l], shift_w[l], decay_w0[l], decay_up[l], iclr_a0[l], iclr_up[l], gate_up[l],
            k_k[l], k_a[l], r_k[l], lnx_g[l], lnx_b[l], pool_w[l], pool_scale[l], fourier_w[l], w_out[l])
        experts = (exp_w_gate[l].astype(BF16), exp_w_up[l].astype(BF16), exp_w_down[l].astype(BF16))
        h_lat = _moe_block(h_lat, norm_ffn_g[l], sh2_l, sc2_l, g2_l, router_w[l], *experts)
        if ctx_out:
            h_ctx = _moe_block(h_ctx, norm_ffn_g[l], sh2_c, sc2_c, g2_c, router_w[l], *experts)
    return _final_norm(h_lat, final_norm_g)
```
